```python
import jax, jax.numpy as jnp
from jax import lax
import numpy as np

D_MODEL = 1024
BATCH = 8
SEQ = 2048
DEPTH = 1
DEC_BATCH = 128
DEC_SEQ = 1
PAST_LEN = 16384
PAGE_SIZE = 128

W_SCONV = D_MODEL
K_SCONV = 3
W_CCONV = D_MODEL
K_CCONV = 31
D_IN_ALL = 3 * W_SCONV + 2 * W_CCONV + 2 * D_MODEL
N_EXPERTS = 64
TOP_K = 8
N_GROUPS = 8
TOPK_GROUPS = 4
D_EXPERT = D_MODEL // 4
D_SHARED = D_EXPERT
ROUTED_SCALE = 2.5
RMS_EPS = 1e-6
LN_EPS = 1e-5

kernel_name = "hybrid_gconv_conformer_moe_step"


def rmsnorm(x, g):
    xf = x.astype(jnp.float32)
    y = xf * lax.rsqrt(jnp.mean(xf * xf, axis=-1, keepdims=True) + RMS_EPS)
    return (y * g.astype(jnp.float32)).astype(x.dtype)


def layernorm(x, g, b):
    xf = x.astype(jnp.float32)
    mu = jnp.mean(xf, axis=-1, keepdims=True)
    var = jnp.mean(jnp.square(xf - mu), axis=-1, keepdims=True)
    y = (xf - mu) * lax.rsqrt(var + LN_EPS) * g.astype(jnp.float32) + b.astype(jnp.float32)
    return y.astype(x.dtype)


def causal_dwconv(u, ctx, w):
    k, ch = w.shape
    full = jnp.concatenate([ctx.astype(u.dtype), u], axis=1)
    y = lax.conv_general_dilated(full, w[:, None, :].astype(u.dtype), window_strides=(1,),
                                 padding='VALID', dimension_numbers=('NWC', 'WIO', 'NWC'),
                                 feature_group_count=ch)
    return y, full[:, full.shape[1] - (k - 1):]


def token_mixer(h, st_a, st_b, p):
    z = h @ p['w_in']
    cut = np.cumsum([W_SCONV, W_SCONV, W_SCONV, W_CCONV, W_CCONV, D_MODEL]).tolist()
    a_in, a_b, a_c, b_val, b_gate, g_a, g_b = jnp.split(z, cut, axis=-1)
    v, new_a = causal_dwconv(a_c * a_in, st_a, p['w_sconv'])
    y_a = (a_b * v) @ p['w_out_a']
    glu = b_val * jax.nn.sigmoid(b_gate)
    cv, new_b = causal_dwconv(glu, st_b, p['w_cconv'])
    cv = cv + p['b_cconv']
    y_b = jax.nn.silu(layernorm(cv, p['ln_g'], p['ln_b'])) @ p['w_out_b'] + p['b_out_b']
    m = jax.nn.sigmoid(g_a) * y_a + jax.nn.sigmoid(g_b) * y_b
    return m @ p['w_o'], new_a, new_b


def moe_tokens(h, p):
    t = h.shape[0]
    scores = jax.nn.sigmoid((h @ p['w_router']).astype(jnp.float32))
    sel = scores + p['e_bias'].astype(jnp.float32)
    gscore = jnp.sum(lax.top_k(sel.reshape(t, N_GROUPS, N_EXPERTS // N_GROUPS), 2)[0], axis=-1)
    _, gidx = lax.top_k(gscore, TOPK_GROUPS)
    gmask = jnp.sum(jax.nn.one_hot(gidx, N_GROUPS, dtype=jnp.float32), axis=1) > 0
    emask = jnp.repeat(gmask, N_EXPERTS // N_GROUPS, axis=1)
    _, eidx = lax.top_k(jnp.where(emask, sel, -jnp.inf), TOP_K)
    w = jnp.take_along_axis(scores, eidx, axis=1)
    w = w / jnp.sum(w, axis=-1, keepdims=True) * ROUTED_SCALE
    gate = jnp.sum(jax.nn.one_hot(eidx, N_EXPERTS, dtype=jnp.float32) * w[..., None], axis=1)
    hg = jnp.einsum('td,edf->tef', h, p['w1'])
    hu = jnp.einsum('td,edf->tef', h, p['w3'])
    act = jax.nn.silu(hg) * hu * gate[:, :, None].astype(h.dtype)
    routed = jnp.einsum('tef,efd->td', act, p['w2'])
    shared = (jax.nn.silu(h @ p['ws1']) * (h @ p['ws3'])) @ p['ws2']
    return routed + shared


def layer(x, c, st_a, st_b, p):
    ada = jax.nn.silu(c) @ p['w_ada'] + p['b_ada']
    sh1, sc1, gt1, sh2, sc2, gt2 = [a[:, None, :] for a in jnp.split(ada, 6, axis=-1)]
    h = rmsnorm(x, p['g_pre1']) * (1 + sc1) + sh1
    t, new_a, new_b = token_mixer(h, st_a, st_b, p)
    x = x + gt1 * rmsnorm(t, p['g_post1'])
    h = rmsnorm(x, p['g_pre2']) * (1 + sc2) + sh2
    f = lax.map(lambda hs: moe_tokens(hs, p), h)
    x = x + gt2 * rmsnorm(f, p['g_post2'])
    return x, new_a, new_b


def setup_inputs(seed: int = 0) -> dict:
    key = jax.random.key(seed)
    ks = list(jax.random.split(key, 40))
    nrm = lambda k, shape, s: jax.random.normal(k, shape, jnp.float32) * s
    L, D, E, F = DEPTH, D_MODEL, N_EXPERTS, D_EXPERT
    return {
        'x_prompt': nrm(ks[0], (BATCH, SEQ, D), 1.0),
        'x_sample': nrm(ks[1], (DEC_BATCH, DEC_SEQ, D), 1.0),
        'state_sconv': nrm(ks[2], (L, DEC_BATCH, K_SCONV - 1, W_SCONV), 1.0),
        'state_cconv': nrm(ks[3], (L, DEC_BATCH, K_CCONV - 1, W_CCONV), 0.5),
        'c_prompt': nrm(ks[4], (BATCH, D), 1.0),
        'c_sample': nrm(ks[5], (DEC_BATCH, D), 1.0),
        'w_ada': nrm(ks[6], (L, D, 6 * D), 0.5 * D ** -0.5),
        'b_ada': nrm(ks[7], (L, 6 * D), 0.02),
        'g_pre1': 1.0 + nrm(ks[8], (L, D), 0.02),
        'g_post1': 1.0 + nrm(ks[9], (L, D), 0.02),
        'g_pre2': 1.0 + nrm(ks[10], (L, D), 0.02),
        'g_post2': 1.0 + nrm(ks[11], (L, D), 0.02),
        'w_in': nrm(ks[12], (L, D, D_IN_ALL), D ** -0.5),
        'w_sconv': nrm(ks[13], (L, K_SCONV, W_SCONV), K_SCONV ** -0.5),
        'w_out_a': nrm(ks[14], (L, W_SCONV, D), W_SCONV ** -0.5),
        'w_cconv': nrm(ks[15], (L, K_CCONV, W_CCONV), K_CCONV ** -0.5),
        'b_cconv': nrm(ks[16], (L, W_CCONV), 0.02),
        'ln_g': 1.0 + nrm(ks[17], (L, W_CCONV), 0.02),
        'ln_b': nrm(ks[18], (L, W_CCONV), 0.02),
        'w_out_b': nrm(ks[19], (L, W_CCONV, D), W_CCONV ** -0.5),
        'b_out_b': nrm(ks[20], (L, D), 0.02),
        'w_o': nrm(ks[21], (L, D, D), D ** -0.5),
        'w_router': nrm(ks[22], (L, D, E), D ** -0.5),
        'e_bias': nrm(ks[23], (L, E), 0.01),
        'w1': nrm(ks[24], (L, E, D, F), D ** -0.5),
        'w3': nrm(ks[25], (L, E, D, F), D ** -0.5),
        'w2': nrm(ks[26], (L, E, F, D), F ** -0.5),
        'ws1': nrm(ks[27], (L, D, D_SHARED), D ** -0.5),
        'ws3': nrm(ks[28], (L, D, D_SHARED), D ** -0.5),
        'ws2': nrm(ks[29], (L, D_SHARED, D), D_SHARED ** -0.5),
    }


def reference(x_prompt, x_sample, state_sconv, state_cconv, c_prompt, c_sample,
              w_ada, b_ada, g_pre1, g_post1, g_pre2, g_post2,
              w_in, w_sconv, w_out_a, w_cconv, b_cconv, ln_g, ln_b, w_out_b, b_out_b, w_o,
              w_router, e_bias, w1, w3, w2, ws1, ws3, ws2):
    yp, ys = x_prompt, x_sample
    sa_p, sa_s, sb_p, sb_s = [], [], [], []
    zero_a = jnp.zeros((x_prompt.shape[0], K_SCONV - 1, W_SCONV), x_prompt.dtype)
    zero_b = jnp.zeros((x_prompt.shape[0], K_CCONV - 1, W_CCONV), x_prompt.dtype)
    for l in range(DEPTH):
        p = {'w_ada': w_ada[l], 'b_ada': b_ada[l], 'g_pre1': g_pre1[l], 'g_post1': g_post1[l],
             'g_pre2': g_pre2[l], 'g_post2': g_post2[l], 'w_in': w_in[l], 'w_sconv': w_sconv[l],
             'w_out_a': w_out_a[l], 'w_cconv': w_cconv[l], 'b_cconv': b_cconv[l], 'ln_g': ln_g[l],
             'ln_b': ln_b[l], 'w_out_b': w_out_b[l], 'b_out_b': b_out_b[l], 'w_o': w_o[l],
             'w_router': w_router[l], 'e_bias': e_bias[l], 'w1': w1[l], 'w3': w3[l], 'w2': w2[l],
             'ws1': ws1[l], 'ws3': ws3[l], 'ws2': ws2[l]}
        yp, na_p, nb_p = layer(yp, c_prompt, zero_a, zero_b, p)
        ys, na_s, nb_s = layer(ys, c_sample, state_sconv[l], state_cconv[l], p)
        sa_p.append(na_p); sa_s.append(na_s); sb_p.append(nb_p); sb_s.append(nb_s)
    new_sconv_prompt = jnp.stack(sa_p, axis=0)
    new_sconv_sample = jnp.stack(sa_s, axis=0)
    new_cconv_prompt = jnp.stack(sb_p, axis=0)
    new_cconv_sample = jnp.stack(sb_s, axis=0)
    return (yp, ys, new_sconv_prompt, new_sconv_sample, new_cconv_prompt, new_cconv_sample)
```

```python
import functools

import jax
import jax.numpy as jnp
from jax import lax
from jax.experimental import pallas as pl
from jax.experimental.pallas import tpu as pltpu

D_MODEL = 1024
K_SCONV = 3
K_CCONV = 31
N_EXPERTS = 64
TOP_K = 8
N_GROUPS = 8
GROUP_SIZE = N_EXPERTS // N_GROUPS
TOPK_GROUPS = 4
D_EXPERT = 256
ROUTED_SCALE = 2.5
RMS_EPS = 1e-6
LN_EPS = 1e-5

F32 = jnp.float32
BF16 = jnp.bfloat16
NEG_INF = float("-inf")

VMEM_LIMIT_BYTES_V7X = 56 * 1024 * 1024
SUBLANES = 8

MIXER_TILE = 512
CONV_HALO = 32
CONV_ROW_CHUNK = 64
CONV_LANE_CHUNK = 256
ROUTER_TILE = 512
MOE_TILE = 1024


def _params(*semantics):
    return pltpu.CompilerParams(dimension_semantics=semantics, vmem_limit_bytes=VMEM_LIMIT_BYTES_V7X)


def _resident(shape):
    zeros = (0,) * len(shape)
    return pl.BlockSpec(shape, lambda *_: zeros, pipeline_mode=pl.Buffered(1))


def _silu(x):
    return x * jax.nn.sigmoid(x)


def _rmsnorm(x, g):
    return x * lax.rsqrt(jnp.mean(x * x, axis=-1, keepdims=True) + RMS_EPS) * g


def _layernorm(x, g, b):
    mu = jnp.mean(x, axis=-1, keepdims=True)
    xc = x - mu
    var = jnp.mean(xc * xc, axis=-1, keepdims=True)
    return xc * lax.rsqrt(var + LN_EPS) * g + b


def _mm(a, w):
    return jnp.dot(a.astype(BF16), w.astype(BF16), preferred_element_type=F32)


def _ada_kernel(c_ref, w_ref, b_ref, o_ref):
    o_ref[...] = _mm(_silu(c_ref[...]), w_ref[...]) + b_ref[...]


def _ada(c, w_ada, b_ada):
    n = c.shape[0]
    width = w_ada.shape[1]
    bn = D_MODEL
    return pl.pallas_call(
        _ada_kernel,
        grid=(width // bn,),
        in_specs=[
            pl.BlockSpec((n, D_MODEL), lambda i: (0, 0)),
            pl.BlockSpec((D_MODEL, bn), lambda i: (0, i)),
            pl.BlockSpec((1, bn), lambda i: (0, i)),
        ],
        out_specs=pl.BlockSpec((n, bn), lambda i: (0, i)),
        out_shape=jax.ShapeDtypeStruct((n, width), F32),
        compiler_params=_params("arbitrary"),
        name="ada",
    )(c, w_ada, b_ada)


def _mixer_tail(x, ada_ref, y_a, y_b, g_a, g_b, w_o_ref, gpost1_ref, gpre2_ref, x1_ref, h2_ref):
    m = jax.nn.sigmoid(g_a) * y_a + jax.nn.sigmoid(g_b) * y_b
    t = _mm(m, w_o_ref[...])
    x1 = x + ada_ref[2, 0] * _rmsnorm(t, gpost1_ref[...])
    h2 = _rmsnorm(x1, gpre2_ref[...]) * (1.0 + ada_ref[4, 0]) + ada_ref[3, 0]
    x1_ref[...] = x1.reshape(x1_ref.shape)
    h2_ref[...] = h2.astype(BF16).reshape(h2_ref.shape)


def _prompt_mixer_kernel(x_ref, ada_ref, gpre1_ref, gpost1_ref, gpre2_ref, w_in_ref, w_sconv_ref,
                         w_out_a_ref, w_cconv_ref, b_cconv_ref, ln_g_ref, ln_b_ref, w_out_b_ref,
                         b_out_b_ref, w_o_ref,
                         x1_ref, h2_ref, u_tail_ref, glu_tail_ref,
                         u_ext, glu_ext, cv_buf):
    tm = x_ref.shape[1]
    d = D_MODEL

    @pl.when(pl.program_id(1) == 0)
    def _():
        u_ext[pl.ds(0, SUBLANES), :] = jnp.zeros((SUBLANES, d), F32)
        glu_ext[pl.ds(0, CONV_HALO), :] = jnp.zeros((CONV_HALO, d), F32)

    x = x_ref[0]
    h = (_rmsnorm(x, gpre1_ref[...]) * (1.0 + ada_ref[1, 0]) + ada_ref[0, 0]).astype(BF16)

    def proj(i):
        return jnp.dot(h, w_in_ref[:, i * d:(i + 1) * d], preferred_element_type=F32)

    u = proj(2) * proj(0)
    u_ext[pl.ds(SUBLANES, tm), :] = u
    v = (w_sconv_ref[0:1, :] * u_ext[pl.ds(SUBLANES - 2, tm), :]
         + w_sconv_ref[1:2, :] * u_ext[pl.ds(SUBLANES - 1, tm), :]
         + w_sconv_ref[2:3, :] * u)
    u_tail_ref[0] = u[tm - SUBLANES:, :]
    u_ext[pl.ds(0, SUBLANES), :] = u[tm - SUBLANES:, :]
    y_a = _mm(proj(1) * v, w_out_a_ref[...])

    glu = proj(3) * jax.nn.sigmoid(proj(4))
    glu_ext[pl.ds(CONV_HALO, tm), :] = glu
    glu_tail_ref[0] = glu[tm - CONV_HALO:, :]
    first = CONV_HALO - (K_CCONV - 1)
    win_rows = CONV_ROW_CHUNK + CONV_HALO + SUBLANES

    def conv_rows(r, carry):
        r0 = pl.multiple_of(r * CONV_ROW_CHUNK, CONV_ROW_CHUNK)
        for c in range(d // CONV_LANE_CHUNK):
            lanes = slice(c * CONV_LANE_CHUNK, (c + 1) * CONV_LANE_CHUNK)
            win = glu_ext[pl.ds(r0, win_rows), lanes]
            acc = jnp.zeros((CONV_ROW_CHUNK, CONV_LANE_CHUNK), F32)
            for res in range(SUBLANES):
                taps = [k for k in range(K_CCONV) if (first + k) % SUBLANES == res]
                if not taps:
                    continue
                sh = win[res:res + CONV_ROW_CHUNK + CONV_HALO, :]
                for k in taps:
                    q = (first + k) // SUBLANES * SUBLANES
                    acc = acc + w_cconv_ref[k:k + 1, lanes] * sh[q:q + CONV_ROW_CHUNK, :]
            cv_buf[pl.ds(r0, CONV_ROW_CHUNK), lanes] = acc
        return carry

    lax.fori_loop(0, tm // CONV_ROW_CHUNK, conv_rows, 0)
    glu_ext[pl.ds(0, CONV_HALO), :] = glu[tm - CONV_HALO:, :]
    cv = cv_buf[...] + b_cconv_ref[...]
    y_b = _mm(_silu(_layernorm(cv, ln_g_ref[...], ln_b_ref[...])), w_out_b_ref[...]) + b_out_b_ref[...]

    _mixer_tail(x, ada_ref, y_a, y_b, proj(5), proj(6), w_o_ref, gpost1_ref, gpre2_ref, x1_ref, h2_ref)


def _prompt_mixer(x, ada4, p):
    nb, seq, d = x.shape
    tm = MIXER_TILE
    row = lambda: _resident((1, d))
    tok = lambda: pl.BlockSpec((1, tm, d), lambda b, j: (b, j, 0))
    return pl.pallas_call(
        _prompt_mixer_kernel,
        grid=(nb, seq // tm),
        in_specs=[
            tok(),
            pl.BlockSpec((6, 1, 1, d), lambda b, j: (0, b, 0, 0)),
            row(), row(), row(),
            _resident(p["w_in"].shape),
            _resident((K_SCONV, d)),
            _resident((d, d)),
            _resident((K_CCONV, d)),
            row(), row(), row(),
            _resident((d, d)),
            row(),
            _resident((d, d)),
        ],
        out_specs=[
            tok(), tok(),
            pl.BlockSpec((1, SUBLANES, d), lambda b, j: (b, 0, 0)),
            pl.BlockSpec((1, CONV_HALO, d), lambda b, j: (b, 0, 0)),
        ],
        out_shape=[
            jax.ShapeDtypeStruct((nb, seq, d), F32),
            jax.ShapeDtypeStruct((nb, seq, d), BF16),
            jax.ShapeDtypeStruct((nb, SUBLANES, d), F32),
            jax.ShapeDtypeStruct((nb, CONV_HALO, d), F32),
        ],
        scratch_shapes=[
            pltpu.VMEM((tm + SUBLANES, d), F32),
            pltpu.VMEM((tm + CONV_HALO + SUBLANES, d), F32),
            pltpu.VMEM((tm, d), F32),
        ],
        compiler_params=_params("arbitrary", "arbitrary"),
        name="prompt_mixer",
    )(x, ada4, p["g_pre1"], p["g_post1"], p["g_pre2"], p["w_in"], p["w_sconv"], p["w_out_a"],
      p["w_cconv"], p["b_cconv"], p["ln_g"], p["ln_b"], p["w_out_b"], p["b_out_b"], p["w_o"])


def _sample_mixer_kernel(x_ref, ada_ref, st_a_ref, st_b_ref, gpre1_ref, gpost1_ref, gpre2_ref, w_in_ref,
                         w_sconv_ref, w_out_a_ref, w_cconv_ref, b_cconv_ref, ln_g_ref, ln_b_ref,
                         w_out_b_ref, b_out_b_ref, w_o_ref,
                         x1_ref, h2_ref, u_ref, glu_ref):
    d = D_MODEL
    x = x_ref[...]
    h = (_rmsnorm(x, gpre1_ref[...]) * (1.0 + ada_ref[1, 0]) + ada_ref[0, 0]).astype(BF16)

    def proj(i):
        return jnp.dot(h, w_in_ref[:, i * d:(i + 1) * d], preferred_element_type=F32)

    u = proj(2) * proj(0)
    u_ref[...] = u
    v = w_sconv_ref[K_SCONV - 1:K_SCONV, :] * u
    for k in range(K_SCONV - 1):
        v = v + w_sconv_ref[k:k + 1, :] * st_a_ref[:, k * d:(k + 1) * d]
    y_a = _mm(proj(1) * v, w_out_a_ref[...])

    glu = proj(3) * jax.nn.sigmoid(proj(4))
    glu_ref[...] = glu
    cv = w_cconv_ref[K_CCONV - 1:K_CCONV, :] * glu + b_cconv_ref[...]
    for k in range(K_CCONV - 1):
        cv = cv + w_cconv_ref[k:k + 1, :] * st_b_ref[:, k * d:(k + 1) * d]
    y_b = _mm(_silu(_layernorm(cv, ln_g_ref[...], ln_b_ref[...])), w_out_b_ref[...]) + b_out_b_ref[...]

    _mixer_tail(x, ada_ref, y_a, y_b, proj(5), proj(6), w_o_ref, gpost1_ref, gpre2_ref, x1_ref, h2_ref)


def _sample_mixer(x, ada4, st_a, st_b, p):
    n, d = x.shape
    tb = 32
    row = lambda: _resident((1, d))
    tok = lambda: pl.BlockSpec((tb, d), lambda i: (i, 0))
    return pl.pallas_call(
        _sample_mixer_kernel,
        grid=(n // tb,),
        in_specs=[
            tok(),
            pl.BlockSpec((6, 1, tb, d), lambda i: (0, 0, i, 0)),
            pl.BlockSpec((tb, (K_SCONV - 1) * d), lambda i: (i, 0)),
            pl.BlockSpec((tb, (K_CCONV - 1) * d), lambda i: (i, 0)),
            row(), row(), row(),
            _resident(p["w_in"].shape),
            _resident((K_SCONV, d)),
            _resident((d, d)),
            _resident((K_CCONV, d)),
            row(), row(), row(),
            _resident((d, d)),
            row(),
            _resident((d, d)),
        ],
        out_specs=[tok(), tok(), tok(), tok()],
        out_shape=[
            jax.ShapeDtypeStruct((n, d), F32),
            jax.ShapeDtypeStruct((n, d), BF16),
            jax.ShapeDtypeStruct((n, d), F32),
            jax.ShapeDtypeStruct((n, d), F32),
        ],
        compiler_params=_params("arbitrary"),
        name="sample_mixer",
    )(x, ada4, st_a, st_b, p["g_pre1"], p["g_post1"], p["g_pre2"], p["w_in"], p["w_sconv"], p["w_out_a"],
      p["w_cconv"], p["b_cconv"], p["ln_g"], p["ln_b"], p["w_out_b"], p["b_out_b"], p["w_o"])


def _first_index_of_max(work, iota, axis, limit):
    m = jnp.max(work, axis=axis, keepdims=True)
    idx = jnp.min(jnp.where(work == m, iota, limit), axis=axis, keepdims=True)
    return m, idx


def _router_kernel(h_ref, wr_ref, bias_ref, gate_ref):
    tm = h_ref.shape[0]
    logits = lax.dot_general(wr_ref[...], h_ref[...], (((1,), (1,)), ((), ())),
                             preferred_element_type=F32)
    scores = jax.nn.sigmoid(logits)
    sel = scores + bias_ref[...]

    sel3 = sel.reshape(N_GROUPS, GROUP_SIZE, tm)
    io3 = lax.broadcasted_iota(jnp.int32, sel3.shape, 1)
    m1, i1 = _first_index_of_max(sel3, io3, 1, GROUP_SIZE)
    m2 = jnp.max(jnp.where(io3 == i1, NEG_INF, sel3), axis=1, keepdims=True)
    gscore = (m1 + m2).reshape(N_GROUPS, tm)

    iog = lax.broadcasted_iota(jnp.int32, gscore.shape, 0)
    gmask = jnp.zeros(gscore.shape, jnp.bool_)
    for _ in range(TOPK_GROUPS):
        _, gi = _first_index_of_max(gscore, iog, 0, N_GROUPS)
        pick = iog == gi
        gmask = jnp.logical_or(gmask, pick)
        gscore = jnp.where(pick, NEG_INF, gscore)

    work = jnp.where(gmask.reshape(N_GROUPS, 1, tm), sel3, NEG_INF).reshape(N_EXPERTS, tm)
    ioe = lax.broadcasted_iota(jnp.int32, work.shape, 0)
    chosen = jnp.zeros(work.shape, jnp.bool_)
    for _ in range(TOP_K):
        _, ei = _first_index_of_max(work, ioe, 0, N_EXPERTS)
        pick = ioe == ei
        chosen = jnp.logical_or(chosen, pick)
        work = jnp.where(pick, NEG_INF, work)

    w = jnp.where(chosen, scores, 0.0)
    gate_t = w / jnp.sum(w, axis=0, keepdims=True) * ROUTED_SCALE
    gate_ref[...] = gate_t.T


def _router(h2, w_router_t, e_bias_col, tm):
    t, d = h2.shape
    return pl.pallas_call(
        _router_kernel,
        grid=(t // tm,),
        in_specs=[
            pl.BlockSpec((tm, d), lambda i: (i, 0)),
            _resident((N_EXPERTS, d)),
            _resident((N_EXPERTS, 1)),
        ],
        out_specs=pl.BlockSpec((tm, N_EXPERTS), lambda i: (i, 0)),
        out_shape=jax.ShapeDtypeStruct((t, N_EXPERTS), F32),
        compiler_params=_params("arbitrary"),
        name="router",
    )(h2, w_router_t, e_bias_col)


def _moe_kernel(h_ref, gate_ref, x1_ref, ada_ref, gpost2_ref, w1_ref, w3_ref, w2_ref,
                ws1_ref, ws3_ref, ws2_ref, out_ref, acc_ref):
    e = pl.program_id(1)
    h = h_ref[...]

    @pl.when(e == 0)
    def _():
        hs = _silu(jnp.dot(h, ws1_ref[...], preferred_element_type=F32)) * jnp.dot(
            h, ws3_ref[...], preferred_element_type=F32)
        acc_ref[...] = _mm(hs, ws2_ref[...])

    lane = lax.broadcasted_iota(jnp.int32, gate_ref.shape, 1)
    g = jnp.sum(jnp.where(lane == e, gate_ref[...], 0.0), axis=1, keepdims=True)
    act = (_silu(jnp.dot(h, w1_ref[0], preferred_element_type=F32))
           * jnp.dot(h, w3_ref[0], preferred_element_type=F32) * g)
    acc_ref[...] += _mm(act, w2_ref[0])

    @pl.when(e == pl.num_programs(1) - 1)
    def _():
        out_ref[...] = x1_ref[...] + ada_ref[5, 0] * _rmsnorm(acc_ref[...], gpost2_ref[...])


def _moe(h2, gate, x1, ada4, tokens_per_ada_block, p, tm):
    t, d = h2.shape
    f = D_EXPERT
    rows = ada4.shape[2]
    tok = lambda: pl.BlockSpec((tm, d), lambda i, e: (i, 0))
    if rows == 1:
        ada_spec = pl.BlockSpec((6, 1, 1, d), lambda i, e: (0, i * tm // tokens_per_ada_block, 0, 0))
    else:
        ada_spec = pl.BlockSpec((6, 1, tm, d), lambda i, e: (0, 0, i, 0))
    return pl.pallas_call(
        _moe_kernel,
        grid=(t // tm, N_EXPERTS),
        in_specs=[
            tok(),
            pl.BlockSpec((tm, N_EXPERTS), lambda i, e: (i, 0)),
            tok(),
            ada_spec,
            _resident((1, d)),
            pl.BlockSpec((1, d, f), lambda i, e: (e, 0, 0)),
            pl.BlockSpec((1, d, f), lambda i, e: (e, 0, 0)),
            pl.BlockSpec((1, f, d), lambda i, e: (e, 0, 0)),
            _resident((d, f)), _resident((d, f)), _resident((f, d)),
        ],
        out_specs=tok(),
        out_shape=jax.ShapeDtypeStruct((t, d), F32),
        scratch_shapes=[pltpu.VMEM((tm, d), F32)],
        compiler_params=_params("arbitrary", "arbitrary"),
        name="moe",
    )(h2, gate, x1, ada4, p["g_post2"], p["w1"], p["w3"], p["w2"], p["ws1"], p["ws3"], p["ws2"])


def kernel(x_prompt, x_sample, state_sconv, state_cconv, c_prompt, c_sample, w_ada, b_ada, g_pre1, g_post1,
           g_pre2, g_post2, w_in, w_sconv, w_out_a, w_cconv, b_cconv, ln_g, ln_b, w_out_b, b_out_b, w_o,
           w_router, e_bias, w1, w3, w2, ws1, ws3, ws2):
    assert w_ada.shape[0] == 1, "single-layer trunk"
    nb, seq, d = x_prompt.shape
    ns = x_sample.shape[0]
    p = {
        "g_pre1": g_pre1, "g_post1": g_post1, "g_pre2": g_pre2, "g_post2": g_post2,
        "w_in": w_in[0].astype(BF16), "w_sconv": w_sconv[0], "w_out_a": w_out_a[0].astype(BF16),
        "w_cconv": w_cconv[0], "b_cconv": b_cconv, "ln_g": ln_g, "ln_b": ln_b,
        "w_out_b": w_out_b[0].astype(BF16), "b_out_b": b_out_b, "w_o": w_o[0].astype(BF16),
        "w1": w1[0].astype(BF16), "w3": w3[0].astype(BF16), "w2": w2[0].astype(BF16),
        "ws1": ws1[0].astype(BF16), "ws3": ws3[0].astype(BF16), "ws2": ws2[0].astype(BF16),
    }
    w_router_t = w_router[0].T.astype(BF16)
    e_bias_col = e_bias[0].reshape(N_EXPERTS, 1)

    ada = _ada(jnp.concatenate([c_prompt, c_sample], axis=0), w_ada[0], b_ada)
    ada = ada.reshape(nb + ns, 6, d).transpose(1, 0, 2)
    ada_p = ada[:, :nb].reshape(6, nb, 1, d)
    ada_s = ada[:, nb:].reshape(6, 1, ns, d)

    x1_p, h2_p, u_tail, glu_tail = _prompt_mixer(x_prompt, ada_p, p)
    x1_s, h2_s, u_s, glu_s = _sample_mixer(
        x_sample.reshape(ns, d), ada_s,
        state_sconv[0].reshape(ns, (K_SCONV - 1) * d), state_cconv[0].reshape(ns, (K_CCONV - 1) * d), p)

    h2_p = h2_p.reshape(nb * seq, d)
    gate_p = _router(h2_p, w_router_t, e_bias_col, ROUTER_TILE)
    gate_s = _router(h2_s, w_router_t, e_bias_col, ns)
    y_p = _moe(h2_p, gate_p, x1_p.reshape(nb * seq, d), ada_p, seq, p, MOE_TILE)
    y_s = _moe(h2_s, gate_s, x1_s, ada_s, 1, p, ns)

    new_sconv_prompt = u_tail[:, SUBLANES - (K_SCONV - 1):][None]
    new_cconv_prompt = glu_tail[:, CONV_HALO - (K_CCONV - 1):][None]
    new_sconv_sample = jnp.concatenate([state_sconv[0][:, 1:], u_s[:, None]], axis=1)[None]
    new_cconv_sample = jnp.concatenate([state_cconv[0][:, 1:], glu_s[:, None]], axis=1)[None]
    return (y_p.reshape(nb, seq, d), y_s.reshape(ns, 1, d),
            new_sconv_prompt, new_sconv_sample, new_cconv_prompt, new_cconv_sample)
```

```python
import functools

import jax
import jax.numpy as jnp
from jax import lax
from jax.experimental import pallas as pl
from jax.experimental.pallas import tpu as pltpu
from jax.experimental.pallas import tpu_sc as plsc

D_MODEL = 1024
K_SCONV = 3
K_CCONV = 31
N_EXPERTS = 64
TOP_K = 8
N_GROUPS = 8
GROUP_SIZE = N_EXPERTS // N_GROUPS
TOPK_GROUPS = 4
D_EXPERT = 256
ROUTED_SCALE = 2.5
RMS_EPS = 1e-6
LN_EPS = 1e-5

F32 = jnp.float32
BF16 = jnp.bfloat16
I32 = jnp.int32
NEG_INF = float("-inf")

VMEM_LIMIT_BYTES_V7X = 56 * 1024 * 1024
SUBLANES = 8
LANES = 128

MIXER_TILE = 512
CONV_HALO = 32
CONV_ROW_CHUNK = 64
CONV_LANE_CHUNK = 256
ROUTER_TILE = 512
POSITION_TILE = 384
EXPERT_TILE = 512
FINAL_TILE = 512
PACKED = D_MODEL // 2
HALF = PACKED // 2
SC_WINDOW = 128


def _params(*semantics):
    return pltpu.CompilerParams(dimension_semantics=semantics, vmem_limit_bytes=VMEM_LIMIT_BYTES_V7X)


def _resident(shape):
    zeros = (0,) * len(shape)
    return pl.BlockSpec(shape, lambda *_: zeros, pipeline_mode=pl.Buffered(1))


def _silu(x):
    return x * jax.nn.sigmoid(x)


def _rmsnorm(x, g):
    return x * lax.rsqrt(jnp.mean(x * x, axis=-1, keepdims=True) + RMS_EPS) * g


def _layernorm(x, g, b):
    mu = jnp.mean(x, axis=-1, keepdims=True)
    xc = x - mu
    var = jnp.mean(xc * xc, axis=-1, keepdims=True)
    return xc * lax.rsqrt(var + LN_EPS) * g + b


def _mm(a, w):
    return jnp.dot(a.astype(BF16), w.astype(BF16), preferred_element_type=F32)


def _pack_bf16_pairs(v):
    n = v.shape[1] // 2
    bits = lax.bitcast_convert_type(v.astype(BF16).astype(F32), I32)
    return bits[:, :n] | lax.shift_right_logical(bits[:, n:], 16)


def _unpack_bf16_pairs(words):
    hi = lax.bitcast_convert_type(words & jnp.int32(-65536), F32)
    lo = lax.bitcast_convert_type(lax.shift_left(words, 16), F32)
    return hi, lo


def _ada_kernel(c_ref, w_ref, b_ref, o_ref):
    o_ref[...] = _mm(_silu(c_ref[...]), w_ref[...]) + b_ref[...]


def _ada(c, w_ada, b_ada):
    n = c.shape[0]
    width = w_ada.shape[1]
    bn = D_MODEL
    return pl.pallas_call(
        _ada_kernel,
        grid=(width // bn,),
        in_specs=[
            pl.BlockSpec((n, D_MODEL), lambda i: (0, 0)),
            pl.BlockSpec((D_MODEL, bn), lambda i: (0, i)),
            pl.BlockSpec((1, bn), lambda i: (0, i)),
        ],
        out_specs=pl.BlockSpec((n, bn), lambda i: (0, i)),
        out_shape=jax.ShapeDtypeStruct((n, width), F32),
        compiler_params=_params("arbitrary"),
        name="ada",
    )(c, w_ada, b_ada)


def _mixer_tail(x, ada_ref, y_a, y_b, g_a, g_b, w_o_ref, gpost1_ref, gpre2_ref, x1_ref, h2_ref, ha_ref, hb_ref):
    m = jax.nn.sigmoid(g_a) * y_a + jax.nn.sigmoid(g_b) * y_b
    t = _mm(m, w_o_ref[...])
    x1 = x + ada_ref[2, 0] * _rmsnorm(t, gpost1_ref[...])
    h2 = _rmsnorm(x1, gpre2_ref[...]) * (1.0 + ada_ref[4, 0]) + ada_ref[3, 0]
    x1_ref[...] = x1.reshape(x1_ref.shape)
    h2_ref[...] = h2.astype(BF16).reshape(h2_ref.shape)
    words = _pack_bf16_pairs(h2)
    ha_ref[...] = words[:, :HALF].reshape(ha_ref.shape)
    hb_ref[...] = words[:, HALF:].reshape(hb_ref.shape)


def _prompt_mixer_kernel(x_ref, ada_ref, gpre1_ref, gpost1_ref, gpre2_ref, w_in_ref, w_sconv_ref,
                         w_out_a_ref, w_cconv_ref, b_cconv_ref, ln_g_ref, ln_b_ref, w_out_b_ref,
                         b_out_b_ref, w_o_ref,
                         x1_ref, h2_ref, ha_ref, hb_ref, u_tail_ref, glu_tail_ref,
                         u_ext, glu_ext, cv_buf):
    tm = x_ref.shape[1]
    d = D_MODEL

    @pl.when(pl.program_id(1) == 0)
    def _():
        u_ext[pl.ds(0, SUBLANES), :] = jnp.zeros((SUBLANES, d), F32)
        glu_ext[pl.ds(0, CONV_HALO), :] = jnp.zeros((CONV_HALO, d), F32)

    x = x_ref[0]
    h = (_rmsnorm(x, gpre1_ref[...]) * (1.0 + ada_ref[1, 0]) + ada_ref[0, 0]).astype(BF16)

    def proj(i):
        return jnp.dot(h, w_in_ref[:, i * d:(i + 1) * d], preferred_element_type=F32)

    u = proj(2) * proj(0)
    u_ext[pl.ds(SUBLANES, tm), :] = u
    v = (w_sconv_ref[0:1, :] * u_ext[pl.ds(SUBLANES - 2, tm), :]
         + w_sconv_ref[1:2, :] * u_ext[pl.ds(SUBLANES - 1, tm), :]
         + w_sconv_ref[2:3, :] * u)
    u_tail_ref[0] = u[tm - SUBLANES:, :]
    u_ext[pl.ds(0, SUBLANES), :] = u[tm - SUBLANES:, :]
    y_a = _mm(proj(1) * v, w_out_a_ref[...])

    glu = proj(3) * jax.nn.sigmoid(proj(4))
    glu_ext[pl.ds(CONV_HALO, tm), :] = glu
    glu_tail_ref[0] = glu[tm - CONV_HALO:, :]
    first = CONV_HALO - (K_CCONV - 1)
    win_rows = CONV_ROW_CHUNK + CONV_HALO + SUBLANES

    def conv_rows(r, carry):
        r0 = pl.multiple_of(r * CONV_ROW_CHUNK, CONV_ROW_CHUNK)
        for c in range(d // CONV_LANE_CHUNK):
            lanes = slice(c * CONV_LANE_CHUNK, (c + 1) * CONV_LANE_CHUNK)
            win = glu_ext[pl.ds(r0, win_rows), lanes]
            acc = jnp.zeros((CONV_ROW_CHUNK, CONV_LANE_CHUNK), F32)
            for res in range(SUBLANES):
                taps = [k for k in range(K_CCONV) if (first + k) % SUBLANES == res]
                if not taps:
                    continue
                sh = win[res:res + CONV_ROW_CHUNK + CONV_HALO, :]
                for k in taps:
                    q = (first + k) // SUBLANES * SUBLANES
                    acc = acc + w_cconv_ref[k:k + 1, lanes] * sh[q:q + CONV_ROW_CHUNK, :]
            cv_buf[pl.ds(r0, CONV_ROW_CHUNK), lanes] = acc
        return carry

    lax.fori_loop(0, tm // CONV_ROW_CHUNK, conv_rows, 0)
    glu_ext[pl.ds(0, CONV_HALO), :] = glu[tm - CONV_HALO:, :]
    cv = cv_buf[...] + b_cconv_ref[...]
    y_b = _mm(_silu(_layernorm(cv, ln_g_ref[...], ln_b_ref[...])), w_out_b_ref[...]) + b_out_b_ref[...]

    _mixer_tail(x, ada_ref, y_a, y_b, proj(5), proj(6), w_o_ref, gpost1_ref, gpre2_ref,
                x1_ref, h2_ref, ha_ref, hb_ref)


def _prompt_mixer(x, ada4, p):
    nb, seq, d = x.shape
    tm = MIXER_TILE
    row = lambda: _resident((1, d))
    tok = lambda w: pl.BlockSpec((1, tm, w), lambda b, j: (b, j, 0))
    return pl.pallas_call(
        _prompt_mixer_kernel,
        grid=(nb, seq // tm),
        in_specs=[
            tok(d),
            pl.BlockSpec((6, 1, 1, d), lambda b, j: (0, b, 0, 0)),
            row(), row(), row(),
            _resident(p["w_in"].shape),
            _resident((K_SCONV, d)),
            _resident((d, d)),
            _resident((K_CCONV, d)),
            row(), row(), row(),
            _resident((d, d)),
            row(),
            _resident((d, d)),
        ],
        out_specs=[
            tok(d), tok(d), tok(HALF), tok(HALF),
            pl.BlockSpec((1, SUBLANES, d), lambda b, j: (b, 0, 0)),
            pl.BlockSpec((1, CONV_HALO, d), lambda b, j: (b, 0, 0)),
        ],
        out_shape=[
            jax.ShapeDtypeStruct((nb, seq, d), F32),
            jax.ShapeDtypeStruct((nb, seq, d), BF16),
            jax.ShapeDtypeStruct((nb, seq, HALF), I32),
            jax.ShapeDtypeStruct((nb, seq, HALF), I32),
            jax.ShapeDtypeStruct((nb, SUBLANES, d), F32),
            jax.ShapeDtypeStruct((nb, CONV_HALO, d), F32),
        ],
        scratch_shapes=[
            pltpu.VMEM((tm + SUBLANES, d), F32),
            pltpu.VMEM((tm + CONV_HALO + SUBLANES, d), F32),
            pltpu.VMEM((tm, d), F32),
        ],
        compiler_params=_params("arbitrary", "arbitrary"),
        name="prompt_mixer",
    )(x, ada4, p["g_pre1"], p["g_post1"], p["g_pre2"], p["w_in"], p["w_sconv"], p["w_out_a"],
      p["w_cconv"], p["b_cconv"], p["ln_g"], p["ln_b"], p["w_out_b"], p["b_out_b"], p["w_o"])


def _sample_mixer_kernel(x_ref, ada_ref, st_a_ref, st_b_ref, gpre1_ref, gpost1_ref, gpre2_ref, w_in_ref,
                         w_sconv_ref, w_out_a_ref, w_cconv_ref, b_cconv_ref, ln_g_ref, ln_b_ref,
                         w_out_b_ref, b_out_b_ref, w_o_ref,
                         x1_ref, h2_ref, ha_ref, hb_ref, u_ref, glu_ref):
    d = D_MODEL
    x = x_ref[...]
    h = (_rmsnorm(x, gpre1_ref[...]) * (1.0 + ada_ref[1, 0]) + ada_ref[0, 0]).astype(BF16)

    def proj(i):
        return jnp.dot(h, w_in_ref[:, i * d:(i + 1) * d], preferred_element_type=F32)

    u = proj(2) * proj(0)
    u_ref[...] = u
    v = w_sconv_ref[K_SCONV - 1:K_SCONV, :] * u
    for k in range(K_SCONV - 1):
        v = v + w_sconv_ref[k:k + 1, :] * st_a_ref[:, k * d:(k + 1) * d]
    y_a = _mm(proj(1) * v, w_out_a_ref[...])

    glu = proj(3) * jax.nn.sigmoid(proj(4))
    glu_ref[...] = glu
    cv = w_cconv_ref[K_CCONV - 1:K_CCONV, :] * glu + b_cconv_ref[...]
    for k in range(K_CCONV - 1):
        cv = cv + w_cconv_ref[k:k + 1, :] * st_b_ref[:, k * d:(k + 1) * d]
    y_b = _mm(_silu(_layernorm(cv, ln_g_ref[...], ln_b_ref[...])), w_out_b_ref[...]) + b_out_b_ref[...]

    _mixer_tail(x, ada_ref, y_a, y_b, proj(5), proj(6), w_o_ref, gpost1_ref, gpre2_ref,
                x1_ref, h2_ref, ha_ref, hb_ref)


def _sample_mixer(x, ada4, st_a, st_b, p):
    n, d = x.shape
    tb = 32
    row = lambda: _resident((1, d))
    tok = lambda w: pl.BlockSpec((tb, w), lambda i: (i, 0))
    return pl.pallas_call(
        _sample_mixer_kernel,
        grid=(n // tb,),
        in_specs=[
            tok(d),
            pl.BlockSpec((6, 1, tb, d), lambda i: (0, 0, i, 0)),
            tok((K_SCONV - 1) * d),
            tok((K_CCONV - 1) * d),
            row(), row(), row(),
            _resident(p["w_in"].shape),
            _resident((K_SCONV, d)),
            _resident((d, d)),
            _resident((K_CCONV, d)),
            row(), row(), row(),
            _resident((d, d)),
            row(),
            _resident((d, d)),
        ],
        out_specs=[tok(d), tok(d), tok(HALF), tok(HALF), tok(d), tok(d)],
        out_shape=[
            jax.ShapeDtypeStruct((n, d), F32),
            jax.ShapeDtypeStruct((n, d), BF16),
            jax.ShapeDtypeStruct((n, HALF), I32),
            jax.ShapeDtypeStruct((n, HALF), I32),
            jax.ShapeDtypeStruct((n, d), F32),
            jax.ShapeDtypeStruct((n, d), F32),
        ],
        compiler_params=_params("arbitrary"),
        name="sample_mixer",
    )(x, ada4, st_a, st_b, p["g_pre1"], p["g_post1"], p["g_pre2"], p["w_in"], p["w_sconv"], p["w_out_a"],
      p["w_cconv"], p["b_cconv"], p["ln_g"], p["ln_b"], p["w_out_b"], p["b_out_b"], p["w_o"])


def _first_index_of_max(work, iota, axis, limit):
    m = jnp.max(work, axis=axis, keepdims=True)
    return m, jnp.min(jnp.where(work == m, iota, limit), axis=axis, keepdims=True)


def _router_kernel(h_ref, wr_ref, bias_ref, eid_ref, w8_ref, cnt_ref):
    tm = h_ref.shape[0]
    logits = lax.dot_general(wr_ref[...], h_ref[...], (((1,), (1,)), ((), ())),
                             preferred_element_type=F32)
    scores = jax.nn.sigmoid(logits)
    sel = scores + bias_ref[...]

    sel3 = sel.reshape(N_GROUPS, GROUP_SIZE, tm)
    io3 = lax.broadcasted_iota(I32, sel3.shape, 1)
    m1, i1 = _first_index_of_max(sel3, io3, 1, GROUP_SIZE)
    m2 = jnp.max(jnp.where(io3 == i1, NEG_INF, sel3), axis=1, keepdims=True)
    gscore = (m1 + m2).reshape(N_GROUPS, tm)

    iog = lax.broadcasted_iota(I32, gscore.shape, 0)
    gmask = jnp.zeros(gscore.shape, jnp.bool_)
    for _ in range(TOPK_GROUPS):
        _, gi = _first_index_of_max(gscore, iog, 0, N_GROUPS)
        pick = iog == gi
        gmask = jnp.logical_or(gmask, pick)
        gscore = jnp.where(pick, NEG_INF, gscore)

    work = jnp.where(gmask.reshape(N_GROUPS, 1, tm), sel3, NEG_INF).reshape(N_EXPERTS, tm)
    ioe = lax.broadcasted_iota(I32, work.shape, 0)
    chosen = jnp.zeros(work.shape, jnp.bool_)
    eids, ws = [], []
    for _ in range(TOP_K):
        _, ei = _first_index_of_max(work, ioe, 0, N_EXPERTS)
        pick = ioe == ei
        eids.append(ei)
        ws.append(jnp.sum(jnp.where(pick, scores, 0.0), axis=0, keepdims=True))
        chosen = jnp.logical_or(chosen, pick)
        work = jnp.where(pick, NEG_INF, work)

    w = jnp.concatenate(ws, axis=0)
    eid_ref[...] = jnp.concatenate(eids, axis=0)
    w8_ref[...] = w / jnp.sum(w, axis=0, keepdims=True) * ROUTED_SCALE

    @pl.when(pl.program_id(0) == 0)
    def _():
        cnt_ref[...] = jnp.zeros(cnt_ref.shape, F32)

    cnt_ref[...] += jnp.sum(chosen.astype(F32), axis=1, keepdims=True)


def _router(h2, w_router_t, e_bias_col, tm):
    t, d = h2.shape
    picks = lambda: pl.BlockSpec((TOP_K, tm), lambda i: (0, i))
    return pl.pallas_call(
        _router_kernel,
        grid=(t // tm,),
        in_specs=[
            pl.BlockSpec((tm, d), lambda i: (i, 0)),
            _resident((N_EXPERTS, d)),
            _resident((N_EXPERTS, 1)),
        ],
        out_specs=[picks(), picks(), pl.BlockSpec((N_EXPERTS, LANES), lambda i: (0, 0))],
        out_shape=[
            jax.ShapeDtypeStruct((TOP_K, t), I32),
            jax.ShapeDtypeStruct((TOP_K, t), F32),
            jax.ShapeDtypeStruct((N_EXPERTS, LANES), F32),
        ],
        compiler_params=_params("arbitrary"),
        name="router",
    )(h2, w_router_t, e_bias_col)


def _position_kernel(eid_ref, off_ref, dest_ref, run_ref):
    tm = eid_ref.shape[1]

    @pl.when(pl.program_id(0) == 0)
    def _():
        run_ref[...] = off_ref[...]

    eid = eid_ref[...]
    ioe = lax.broadcasted_iota(I32, (N_EXPERTS, tm), 0)
    picks = [ioe == eid[k:k + 1, :] for k in range(TOP_K)]
    chosen = functools.reduce(jnp.logical_or, picks)
    before = (lax.broadcasted_iota(I32, (tm, tm), 0) < lax.broadcasted_iota(I32, (tm, tm), 1))
    rank = jnp.dot(chosen.astype(BF16), before.astype(BF16), preferred_element_type=F32)
    pos = run_ref[...] + rank
    dest = [jnp.sum(jnp.where(pk, pos, 0.0), axis=0, keepdims=True) for pk in picks]
    dest_ref[...] = jnp.concatenate(dest, axis=0).astype(I32)
    run_ref[...] += jnp.sum(chosen.astype(F32), axis=1, keepdims=True)


def _positions(eid, off_col):
    t = eid.shape[1]
    tm = POSITION_TILE
    return pl.pallas_call(
        _position_kernel,
        grid=(t // tm,),
        in_specs=[pl.BlockSpec((TOP_K, tm), lambda i: (0, i)), _resident((N_EXPERTS, 1))],
        out_specs=pl.BlockSpec((TOP_K, tm), lambda i: (0, i)),
        out_shape=jax.ShapeDtypeStruct((TOP_K, t), I32),
        scratch_shapes=[pltpu.VMEM((N_EXPERTS, 1), F32)],
        compiler_params=_params("arbitrary"),
        name="positions",
    )(eid, off_col)


def _sc_mesh():
    return plsc.VectorSubcoreMesh(core_axis_name="core", subcore_axis_name="subcore")


def _dispatch_rows(x, dest, n_rows):
    t, width = x.shape
    n_picks = dest.shape[0]

    @functools.partial(pl.kernel, out_type=jax.ShapeDtypeStruct((n_rows, width), x.dtype),
                       mesh=_sc_mesh(), scratch_types=[])
    def scatter(x_hbm, i_hbm, o_hbm):
        def body(x_vmem, i_vmem):
            for k in range(n_picks):
                pltpu.sync_copy(x_vmem, o_hbm.at[i_vmem.at[k]])

        pltpu.emit_pipeline(
            body,
            grid=(t // SC_WINDOW,),
            in_specs=[pl.BlockSpec((SC_WINDOW, width), lambda i: (i, 0)),
                      pl.BlockSpec((n_picks, SC_WINDOW), lambda i: (0, i))],
            out_specs=[],
            core_axis_name=("core", "subcore"),
            dimension_semantics=(pltpu.PARALLEL,),
        )(x_hbm, i_hbm)

    return scatter(x, dest)


def _gather_rows(y, idx):
    n = idx.shape[1]
    width = y.shape[1]

    @functools.partial(pl.kernel, out_type=jax.ShapeDtypeStruct((n, width), y.dtype),
                       mesh=_sc_mesh(), scratch_types=[])
    def gather(y_hbm, i_hbm, o_hbm):
        def body(i_vmem, o_vmem):
            pltpu.sync_copy(y_hbm.at[i_vmem.at[0]], o_vmem)

        pltpu.emit_pipeline(
            body,
            grid=(n // SC_WINDOW,),
            in_specs=[pl.BlockSpec((1, SC_WINDOW), lambda i: (0, i))],
            out_specs=[pl.BlockSpec((SC_WINDOW, width), lambda i: (i, 0))],
            core_axis_name=("core", "subcore"),
            dimension_semantics=(pltpu.PARALLEL,),
        )(i_hbm, o_hbm)

    return gather(y, idx)


def _expert_kernel(te_ref, nvalid_ref, xa_ref, xb_ref, w1_ref, w3_ref, w2_ref, ya_ref, yb_ref,
                   w1_bf, w3_bf, w2_bf):
    g = pl.program_id(0)

    @pl.when(g < nvalid_ref[0])
    def _():
        @pl.when(jnp.logical_or(g == 0, te_ref[g] != te_ref[jnp.maximum(g - 1, 0)]))
        def _():
            w1_bf[...] = w1_ref[0].astype(BF16)
            w3_bf[...] = w3_ref[0].astype(BF16)
            w2_bf[...] = w2_ref[0].astype(BF16)

        hi, lo = _unpack_bf16_pairs(jnp.concatenate([xa_ref[...], xb_ref[...]], axis=1))
        hi = hi.astype(BF16)
        lo = lo.astype(BF16)

        def up(w_bf):
            return (jnp.dot(hi, w_bf[:PACKED, :], preferred_element_type=F32)
                    + jnp.dot(lo, w_bf[PACKED:, :], preferred_element_type=F32))

        act = _silu(up(w1_bf)) * up(w3_bf)
        words = _pack_bf16_pairs(_mm(act, w2_bf[...]))
        ya_ref[...] = words[:, :HALF]
        yb_ref[...] = words[:, HALF:]


def _experts(xa, xb, tile_expert, n_valid, w1, w3, w2):
    rows = xa.shape[0]
    d, f = D_MODEL, D_EXPERT
    tm = EXPERT_TILE
    rows_spec = lambda: pl.BlockSpec((tm, HALF), lambda g, te, nv: (jnp.minimum(g, nv[0] - 1), 0))
    return pl.pallas_call(
        _expert_kernel,
        grid_spec=pltpu.PrefetchScalarGridSpec(
            num_scalar_prefetch=2,
            grid=(rows // tm,),
            in_specs=[
                rows_spec(), rows_spec(),
                pl.BlockSpec((1, d, f), lambda g, te, nv: (te[g], 0, 0)),
                pl.BlockSpec((1, d, f), lambda g, te, nv: (te[g], 0, 0)),
                pl.BlockSpec((1, f, d), lambda g, te, nv: (te[g], 0, 0)),
            ],
            out_specs=[rows_spec(), rows_spec()],
            scratch_shapes=[pltpu.VMEM((d, f), BF16), pltpu.VMEM((d, f), BF16), pltpu.VMEM((f, d), BF16)],
        ),
        out_shape=[jax.ShapeDtypeStruct((rows, HALF), I32), jax.ShapeDtypeStruct((rows, HALF), I32)],
        compiler_params=_params("arbitrary"),
        name="experts",
    )(tile_expert, n_valid, xa, xb, w1, w3, w2)


def _final_kernel(h_ref, x1_ref, ada_ref, gpost2_ref, w8_ref, yga_ref, ygb_ref, ws1_ref, ws3_ref, ws2_ref,
                  out_ref):
    h = h_ref[...]
    hs = _silu(jnp.dot(h, ws1_ref[...], preferred_element_type=F32)) * jnp.dot(
        h, ws3_ref[...], preferred_element_type=F32)
    f = _mm(hs, ws2_ref[...])
    for k in range(TOP_K):
        hi, lo = _unpack_bf16_pairs(jnp.concatenate([yga_ref[k], ygb_ref[k]], axis=1))
        f = f + w8_ref[:, k:k + 1] * jnp.concatenate([hi, lo], axis=1)
    out_ref[...] = x1_ref[...] + ada_ref[5, 0] * _rmsnorm(f, gpost2_ref[...])


def _final(h2, x1, ada4, tokens_per_ada_block, w8_t, yga, ygb, first_token, p, tm):
    t, d = h2.shape
    f = D_EXPERT
    first_block = first_token // tm
    tok = lambda: pl.BlockSpec((tm, d), lambda i: (i, 0))
    if ada4.shape[2] == 1:
        ada_spec = pl.BlockSpec((6, 1, 1, d), lambda i: (0, i * tm // tokens_per_ada_block, 0, 0))
    else:
        ada_spec = pl.BlockSpec((6, 1, tm, d), lambda i: (0, 0, i, 0))
    gathered = lambda: pl.BlockSpec((TOP_K, tm, HALF), lambda i: (0, first_block + i, 0))
    return pl.pallas_call(
        _final_kernel,
        grid=(t // tm,),
        in_specs=[
            tok(), tok(), ada_spec, _resident((1, d)),
            pl.BlockSpec((tm, TOP_K), lambda i: (first_block + i, 0)),
            gathered(), gathered(),
            _resident((d, f)), _resident((d, f)), _resident((f, d)),
        ],
        out_specs=tok(),
        out_shape=jax.ShapeDtypeStruct((t, d), F32),
        compiler_params=_params("arbitrary"),
        name="final",
    )(h2, x1, ada4, p["g_post2"], w8_t, yga, ygb, p["ws1"], p["ws3"], p["ws2"])


def kernel(x_prompt, x_sample, state_sconv, state_cconv, c_prompt, c_sample, w_ada, b_ada, g_pre1, g_post1,
           g_pre2, g_post2, w_in, w_sconv, w_out_a, w_cconv, b_cconv, ln_g, ln_b, w_out_b, b_out_b, w_o,
           w_router, e_bias, w1, w3, w2, ws1, ws3, ws2):
    assert w_ada.shape[0] == 1, "single-layer trunk"
    nb, seq, d = x_prompt.shape
    ns = x_sample.shape[0]
    n_prompt = nb * seq
    n_tokens = n_prompt + ns
    p = {
        "g_pre1": g_pre1, "g_post1": g_post1, "g_pre2": g_pre2, "g_post2": g_post2,
        "w_in": w_in[0].astype(BF16), "w_sconv": w_sconv[0], "w_out_a": w_out_a[0].astype(BF16),
        "w_cconv": w_cconv[0], "b_cconv": b_cconv, "ln_g": ln_g, "ln_b": ln_b,
        "w_out_b": w_out_b[0].astype(BF16), "b_out_b": b_out_b, "w_o": w_o[0].astype(BF16),
        "ws1": ws1[0].astype(BF16), "ws3": ws3[0].astype(BF16), "ws2": ws2[0].astype(BF16),
    }
    w_router_t = w_router[0].T.astype(BF16)
    e_bias_col = e_bias[0].reshape(N_EXPERTS, 1)

    ada = _ada(jnp.concatenate([c_prompt, c_sample], axis=0), w_ada[0], b_ada)
    ada = ada.reshape(nb + ns, 6, d).transpose(1, 0, 2)
    ada_p = ada[:, :nb].reshape(6, nb, 1, d)
    ada_s = ada[:, nb:].reshape(6, 1, ns, d)

    x1_p, h2_p, ha_p, hb_p, u_tail, glu_tail = _prompt_mixer(x_prompt, ada_p, p)
    x1_s, h2_s, ha_s, hb_s, u_s, glu_s = _sample_mixer(
        x_sample.reshape(ns, d), ada_s,
        state_sconv[0].reshape(ns, (K_SCONV - 1) * d), state_cconv[0].reshape(ns, (K_CCONV - 1) * d), p)
    h2_p = h2_p.reshape(n_prompt, d)

    eid_p, w8_p, cnt_p = _router(h2_p, w_router_t, e_bias_col, ROUTER_TILE)
    eid_s, w8_s, cnt_s = _router(h2_s, w_router_t, e_bias_col, ns)
    eid = jnp.concatenate([eid_p, eid_s], axis=1)
    w8_t = jnp.concatenate([w8_p, w8_s], axis=1).T
    counts = (cnt_p[:, 0] + cnt_s[:, 0]).astype(I32)
    padded = (counts + EXPERT_TILE - 1) // EXPERT_TILE * EXPERT_TILE
    ends = jnp.cumsum(padded)
    n_rows = n_tokens * TOP_K + N_EXPERTS * EXPERT_TILE
    tile_start = jnp.arange(n_rows // EXPERT_TILE, dtype=I32) * EXPERT_TILE
    tile_expert = jnp.minimum(jnp.sum(tile_start[:, None] >= ends[None, :], axis=1), N_EXPERTS - 1).astype(I32)
    n_valid = (ends[-1:] // EXPERT_TILE).astype(I32)
    dest = _positions(eid, (ends - padded).astype(F32).reshape(N_EXPERTS, 1))

    ha = jnp.concatenate([ha_p.reshape(n_prompt, HALF), ha_s], axis=0)
    hb = jnp.concatenate([hb_p.reshape(n_prompt, HALF), hb_s], axis=0)
    xa = _dispatch_rows(ha, dest, n_rows)
    xb = _dispatch_rows(hb, dest, n_rows)
    ya, yb = _experts(xa, xb, tile_expert, n_valid, w1[0], w3[0], w2[0])
    flat = dest.reshape(1, TOP_K * n_tokens)
    yga = _gather_rows(ya, flat).reshape(TOP_K, n_tokens, HALF)
    ygb = _gather_rows(yb, flat).reshape(TOP_K, n_tokens, HALF)

    y_p = _final(h2_p, x1_p.reshape(n_prompt, d), ada_p, seq, w8_t, yga, ygb, 0, p, FINAL_TILE)
    y_s = _final(h2_s, x1_s, ada_s, 1, w8_t, yga, ygb, n_prompt, p, ns)

    new_sconv_prompt = u_tail[:, SUBLANES - (K_SCONV - 1):][None]
    new_cconv_prompt = glu_tail[:, CONV_HALO - (K_CCONV - 1):][None]
    new_sconv_sample = jnp.concatenate([state_sconv[0][:, 1:], u_s[:, None]], axis=1)[None]
    new_cconv_sample = jnp.concatenate([state_cconv[0][:, 1:], glu_s[:, None]], axis=1)[None]
    return (y_p.reshape(nb, seq, d), y_s.reshape(ns, 1, d),
            new_sconv_prompt, new_sconv_sample, new_cconv_prompt, new_cconv_sample)
```

```python
import functools

import jax
import jax.numpy as jnp
from jax import lax
from jax.experimental import pallas as pl
from jax.experimental.pallas import tpu as pltpu
from jax.experimental.pallas import tpu_sc as plsc

D_MODEL = 1024
K_SCONV = 3
K_CCONV = 31
N_EXPERTS = 64
TOP_K = 8
N_GROUPS = 8
GROUP_SIZE = N_EXPERTS // N_GROUPS
TOPK_GROUPS = 4
D_EXPERT = 256
ROUTED_SCALE = 2.5
RMS_EPS = 1e-6
LN_EPS = 1e-5

F32 = jnp.float32
BF16 = jnp.bfloat16
I32 = jnp.int32
NEG_INF = float("-inf")

VMEM_LIMIT_BYTES_V7X = 56 * 1024 * 1024
SUBLANES = 8
LANES = 128

MIXER_TILE = 512
CONV_HALO = 32
CONV_ROW_CHUNK = 64
CONV_LANE_CHUNK = 128
PROJ_LANE_CHUNK = 256
ROUTER_TILE = 512
POSITION_TILE = 384
EXPERT_TILE = 512
FINAL_TILE = 512
PACKED = D_MODEL // 2
HALF = PACKED // 2
SC_WINDOW = 128


def _params(*semantics, flags=None):
    return pltpu.CompilerParams(dimension_semantics=semantics, vmem_limit_bytes=VMEM_LIMIT_BYTES_V7X, flags=flags)


def _resident(shape):
    zeros = (0,) * len(shape)
    return pl.BlockSpec(shape, lambda *_: zeros, pipeline_mode=pl.Buffered(1))


def _silu(x):
    return x * jax.nn.sigmoid(x)


def _rmsnorm(x, g):
    return x * lax.rsqrt(jnp.mean(x * x, axis=-1, keepdims=True) + RMS_EPS) * g


def _layernorm(x, g, b):
    mu = jnp.mean(x, axis=-1, keepdims=True)
    xc = x - mu
    var = jnp.mean(xc * xc, axis=-1, keepdims=True)
    return xc * lax.rsqrt(var + LN_EPS) * g + b


def _mm(a, w):
    return jnp.dot(a.astype(BF16), w.astype(BF16), preferred_element_type=F32)


def _pack_bf16_pairs(v):
    n = v.shape[1] // 2
    bits = lax.bitcast_convert_type(v.astype(BF16).astype(F32), I32)
    return bits[:, :n] | lax.shift_right_logical(bits[:, n:], 16)


def _unpack_bf16_pairs(words):
    hi = lax.bitcast_convert_type(words & jnp.int32(-65536), F32)
    lo = lax.bitcast_convert_type(lax.shift_left(words, 16), F32)
    return hi, lo


def _ada_kernel(c_ref, w_ref, b_ref, o_ref):
    o_ref[...] = _mm(_silu(c_ref[...]), w_ref[...]) + b_ref[...]


def _ada(c, w_ada, b_ada):
    n = c.shape[0]
    width = w_ada.shape[1]
    bn = D_MODEL
    return pl.pallas_call(
        _ada_kernel,
        grid=(width // bn,),
        in_specs=[
            pl.BlockSpec((n, D_MODEL), lambda i: (0, 0)),
            pl.BlockSpec((D_MODEL, bn), lambda i: (0, i)),
            pl.BlockSpec((1, bn), lambda i: (0, i)),
        ],
        out_specs=pl.BlockSpec((n, bn), lambda i: (0, i)),
        out_shape=jax.ShapeDtypeStruct((n, width), F32),
        compiler_params=_params("arbitrary"),
        name="ada",
    )(c, w_ada, b_ada)


def _mixer_tail(x, ada_ref, y_a, y_b, gate_a, gate_b, w_o_ref, gpost1_ref, gpre2_ref, x1_ref, ha_ref, hb_ref):
    m = gate_a * y_a + gate_b * y_b
    t = _mm(m, w_o_ref[...])
    x1 = x + ada_ref[2, 0] * _rmsnorm(t, gpost1_ref[...])
    h2 = _rmsnorm(x1, gpre2_ref[...]) * (1.0 + ada_ref[4, 0]) + ada_ref[3, 0]
    x1_ref[...] = x1.reshape(x1_ref.shape)
    words = _pack_bf16_pairs(h2)
    ha_ref[...] = words[:, :HALF].reshape(ha_ref.shape)
    hb_ref[...] = words[:, HALF:].reshape(hb_ref.shape)


def _prompt_mixer_kernel(x_ref, ada_ref, gpre1_ref, gpost1_ref, gpre2_ref, w_in_ref, w_sconv_ref,
                         w_out_a_ref, w_cconv_ref, b_cconv_ref, ln_g_ref, ln_b_ref, w_out_b_ref,
                         b_out_b_ref, w_o_ref,
                         x1_ref, ha_ref, hb_ref, u_tail_ref, glu_tail_ref,
                         u_ext, glu_ext, cv_buf, h_buf, av_buf, ga_buf, gb_buf):
    tm = x_ref.shape[1]
    d = D_MODEL

    @pl.when(pl.program_id(1) == 0)
    def _():
        u_ext[pl.ds(0, SUBLANES), :] = jnp.zeros((SUBLANES, d), F32)
        glu_ext[pl.ds(0, CONV_HALO), :] = jnp.zeros((CONV_HALO, d), F32)

    h_buf[...] = (_rmsnorm(x_ref[0], gpre1_ref[...]) * (1.0 + ada_ref[1, 0]) + ada_ref[0, 0]).astype(BF16)

    def proj(i, lanes=slice(0, d)):
        return jnp.dot(h_buf[...], w_in_ref[:, i * d + lanes.start:i * d + lanes.stop],
                       preferred_element_type=F32)

    glu = proj(3) * jax.nn.sigmoid(proj(4))
    glu_ext[pl.ds(CONV_HALO, tm), :] = glu
    glu_tail_ref[0] = glu[tm - CONV_HALO:, :]
    first = CONV_HALO - (K_CCONV - 1)

    def conv_piece(r0, lanes):
        acc = None
        for res in range(SUBLANES):
            rows = CONV_ROW_CHUNK + (SUBLANES if res else 0)
            part = None
            for k in range(K_CCONV):
                if (first + k) % SUBLANES != res:
                    continue
                start = r0 + first + k - res
                term = w_cconv_ref[k:k + 1, lanes] * glu_ext[start:start + rows, lanes]
                part = term if part is None else part + term
            part = part[res:res + CONV_ROW_CHUNK, :]
            acc = part if acc is None else acc + part
        cv_buf[r0:r0 + CONV_ROW_CHUNK, lanes] = acc

    def branch_a(lanes):
        u = proj(2, lanes) * proj(0, lanes)
        u_ext[pl.ds(SUBLANES, tm), lanes] = u
        v = (w_sconv_ref[0:1, lanes] * u_ext[pl.ds(SUBLANES - 2, tm), lanes]
             + w_sconv_ref[1:2, lanes] * u_ext[pl.ds(SUBLANES - 1, tm), lanes]
             + w_sconv_ref[2:3, lanes] * u)
        u_tail_ref[0, :, lanes] = u[tm - SUBLANES:, :]
        u_ext[pl.ds(0, SUBLANES), lanes] = u[tm - SUBLANES:, :]
        av_buf[:, lanes] = (proj(1, lanes) * v).astype(BF16)

    def gates(lanes):
        ga_buf[:, lanes] = jax.nn.sigmoid(proj(5, lanes))
        gb_buf[:, lanes] = jax.nn.sigmoid(proj(6, lanes))

    for c in range(0, d, PROJ_LANE_CHUNK):
        lanes = slice(c, c + PROJ_LANE_CHUNK)

        @pl.when(pl.program_id(1) >= 0)
        def _(lanes=lanes):
            for r0 in range(0, tm, CONV_ROW_CHUNK):
                for l0 in range(lanes.start, lanes.stop, CONV_LANE_CHUNK):
                    conv_piece(r0, slice(l0, l0 + CONV_LANE_CHUNK))
            branch_a(lanes)
            gates(lanes)

    glu_ext[pl.ds(0, CONV_HALO), :] = glu_ext[pl.ds(tm, CONV_HALO), :]

    y_a = jnp.dot(av_buf[...], w_out_a_ref[...], preferred_element_type=F32)
    cv = cv_buf[...] + b_cconv_ref[...]
    y_b = _mm(_silu(_layernorm(cv, ln_g_ref[...], ln_b_ref[...])), w_out_b_ref[...]) + b_out_b_ref[...]

    _mixer_tail(x_ref[0], ada_ref, y_a, y_b, ga_buf[...], gb_buf[...], w_o_ref, gpost1_ref, gpre2_ref,
                x1_ref, ha_ref, hb_ref)


def _prompt_mixer(x, ada4, p):
    nb, seq, d = x.shape
    tm = MIXER_TILE
    row = lambda: _resident((1, d))
    tok = lambda w: pl.BlockSpec((1, tm, w), lambda b, j: (b, j, 0))
    return pl.pallas_call(
        _prompt_mixer_kernel,
        grid=(nb, seq // tm),
        in_specs=[
            tok(d),
            pl.BlockSpec((6, 1, 1, d), lambda b, j: (0, b, 0, 0)),
            row(), row(), row(),
            _resident(p["w_in"].shape),
            _resident((K_SCONV, d)),
            _resident((d, d)),
            _resident((K_CCONV, d)),
            row(), row(), row(),
            _resident((d, d)),
            row(),
            _resident((d, d)),
        ],
        out_specs=[
            tok(d), tok(HALF), tok(HALF),
            pl.BlockSpec((1, SUBLANES, d), lambda b, j: (b, 0, 0)),
            pl.BlockSpec((1, CONV_HALO, d), lambda b, j: (b, 0, 0)),
        ],
        out_shape=[
            jax.ShapeDtypeStruct((nb, seq, d), F32),
            jax.ShapeDtypeStruct((nb, seq, HALF), I32),
            jax.ShapeDtypeStruct((nb, seq, HALF), I32),
            jax.ShapeDtypeStruct((nb, SUBLANES, d), F32),
            jax.ShapeDtypeStruct((nb, CONV_HALO, d), F32),
        ],
        scratch_shapes=[
            pltpu.VMEM((tm + SUBLANES, d), F32),
            pltpu.VMEM((tm + CONV_HALO + SUBLANES, d), F32),
            pltpu.VMEM((tm, d), F32),
            pltpu.VMEM((tm, d), BF16),
            pltpu.VMEM((tm, d), BF16),
            pltpu.VMEM((tm, d), F32),
            pltpu.VMEM((tm, d), F32),
        ],
        compiler_params=_params("arbitrary", "arbitrary"),
        name="prompt_mixer",
    )(x, ada4, p["g_pre1"], p["g_post1"], p["g_pre2"], p["w_in"], p["w_sconv"], p["w_out_a"],
      p["w_cconv"], p["b_cconv"], p["ln_g"], p["ln_b"], p["w_out_b"], p["b_out_b"], p["w_o"])


def _sample_mixer_kernel(x_ref, ada_ref, st_a_ref, st_b_ref, gpre1_ref, gpost1_ref, gpre2_ref, w_in_ref,
                         w_sconv_ref, w_out_a_ref, w_cconv_ref, b_cconv_ref, ln_g_ref, ln_b_ref,
                         w_out_b_ref, b_out_b_ref, w_o_ref,
                         x1_ref, ha_ref, hb_ref, u_ref, glu_ref):
    d = D_MODEL
    x = x_ref[...]
    h = (_rmsnorm(x, gpre1_ref[...]) * (1.0 + ada_ref[1, 0]) + ada_ref[0, 0]).astype(BF16)

    def proj(i):
        return jnp.dot(h, w_in_ref[:, i * d:(i + 1) * d], preferred_element_type=F32)

    u = proj(2) * proj(0)
    u_ref[...] = u
    v = w_sconv_ref[K_SCONV - 1:K_SCONV, :] * u
    for k in range(K_SCONV - 1):
        v = v + w_sconv_ref[k:k + 1, :] * st_a_ref[:, k * d:(k + 1) * d]
    y_a = _mm(proj(1) * v, w_out_a_ref[...])

    glu = proj(3) * jax.nn.sigmoid(proj(4))
    glu_ref[...] = glu
    cv = w_cconv_ref[K_CCONV - 1:K_CCONV, :] * glu + b_cconv_ref[...]
    for k in range(K_CCONV - 1):
        cv = cv + w_cconv_ref[k:k + 1, :] * st_b_ref[:, k * d:(k + 1) * d]
    y_b = _mm(_silu(_layernorm(cv, ln_g_ref[...], ln_b_ref[...])), w_out_b_ref[...]) + b_out_b_ref[...]

    _mixer_tail(x, ada_ref, y_a, y_b, jax.nn.sigmoid(proj(5)), jax.nn.sigmoid(proj(6)), w_o_ref, gpost1_ref,
                gpre2_ref, x1_ref, ha_ref, hb_ref)


def _sample_mixer(x, ada4, st_a, st_b, p):
    n, d = x.shape
    tb = 32
    row = lambda: _resident((1, d))
    tok = lambda w: pl.BlockSpec((tb, w), lambda i: (i, 0))
    return pl.pallas_call(
        _sample_mixer_kernel,
        grid=(n // tb,),
        in_specs=[
            tok(d),
            pl.BlockSpec((6, 1, tb, d), lambda i: (0, 0, i, 0)),
            tok((K_SCONV - 1) * d),
            tok((K_CCONV - 1) * d),
            row(), row(), row(),
            _resident(p["w_in"].shape),
            _resident((K_SCONV, d)),
            _resident((d, d)),
            _resident((K_CCONV, d)),
            row(), row(), row(),
            _resident((d, d)),
            row(),
            _resident((d, d)),
        ],
        out_specs=[tok(d), tok(HALF), tok(HALF), tok(d), tok(d)],
        out_shape=[
            jax.ShapeDtypeStruct((n, d), F32),
            jax.ShapeDtypeStruct((n, HALF), I32),
            jax.ShapeDtypeStruct((n, HALF), I32),
            jax.ShapeDtypeStruct((n, d), F32),
            jax.ShapeDtypeStruct((n, d), F32),
        ],
        compiler_params=_params("arbitrary"),
        name="sample_mixer",
    )(x, ada4, st_a, st_b, p["g_pre1"], p["g_post1"], p["g_pre2"], p["w_in"], p["w_sconv"], p["w_out_a"],
      p["w_cconv"], p["b_cconv"], p["ln_g"], p["ln_b"], p["w_out_b"], p["b_out_b"], p["w_o"])


def _first_index_of_max(work, iota, axis, limit):
    m = jnp.max(work, axis=axis, keepdims=True)
    return m, jnp.min(jnp.where(work == m, iota, limit), axis=axis, keepdims=True)


def _router_kernel(ha_ref, hb_ref, wr_ref, bias_ref, eid_ref, w8_ref, cnt_ref):
    tm = ha_ref.shape[0]
    hi, lo = _unpack_bf16_pairs(jnp.concatenate([ha_ref[...], hb_ref[...]], axis=1))
    nt = (((1,), (1,)), ((), ()))
    logits = (lax.dot_general(wr_ref[:, :PACKED], hi.astype(BF16), nt, preferred_element_type=F32)
              + lax.dot_general(wr_ref[:, PACKED:], lo.astype(BF16), nt, preferred_element_type=F32))
    scores = jax.nn.sigmoid(logits)
    sel = scores + bias_ref[...]

    sel3 = sel.reshape(N_GROUPS, GROUP_SIZE, tm)
    io3 = lax.broadcasted_iota(I32, sel3.shape, 1)
    m1, i1 = _first_index_of_max(sel3, io3, 1, GROUP_SIZE)
    m2 = jnp.max(jnp.where(io3 == i1, NEG_INF, sel3), axis=1, keepdims=True)
    gscore = (m1 + m2).reshape(N_GROUPS, tm)

    iog = lax.broadcasted_iota(I32, gscore.shape, 0)
    gmask = jnp.zeros(gscore.shape, jnp.bool_)
    for _ in range(TOPK_GROUPS):
        _, gi = _first_index_of_max(gscore, iog, 0, N_GROUPS)
        pick = iog == gi
        gmask = jnp.logical_or(gmask, pick)
        gscore = jnp.where(pick, NEG_INF, gscore)

    work = jnp.where(gmask.reshape(N_GROUPS, 1, tm), sel3, NEG_INF).reshape(N_EXPERTS, tm)
    ioe = lax.broadcasted_iota(I32, work.shape, 0)
    chosen = jnp.zeros(work.shape, jnp.bool_)
    eids, ws = [], []
    for _ in range(TOP_K):
        _, ei = _first_index_of_max(work, ioe, 0, N_EXPERTS)
        pick = ioe == ei
        eids.append(ei)
        ws.append(jnp.sum(jnp.where(pick, scores, 0.0), axis=0, keepdims=True))
        chosen = jnp.logical_or(chosen, pick)
        work = jnp.where(pick, NEG_INF, work)

    w = jnp.concatenate(ws, axis=0)
    eid_ref[...] = jnp.concatenate(eids, axis=0)
    w8_ref[...] = w / jnp.sum(w, axis=0, keepdims=True) * ROUTED_SCALE

    @pl.when(pl.program_id(0) == 0)
    def _():
        cnt_ref[...] = jnp.zeros(cnt_ref.shape, F32)

    cnt_ref[...] += jnp.sum(chosen.astype(F32), axis=1, keepdims=True)


def _router(ha, hb, w_router_t, e_bias_col, tm):
    t = ha.shape[0]
    d = D_MODEL
    picks = lambda: pl.BlockSpec((TOP_K, tm), lambda i: (0, i))
    half_rows = lambda: pl.BlockSpec((tm, HALF), lambda i: (i, 0))
    return pl.pallas_call(
        _router_kernel,
        grid=(t // tm,),
        in_specs=[
            half_rows(), half_rows(),
            _resident((N_EXPERTS, d)),
            _resident((N_EXPERTS, 1)),
        ],
        out_specs=[picks(), picks(), pl.BlockSpec((N_EXPERTS, LANES), lambda i: (0, 0))],
        out_shape=[
            jax.ShapeDtypeStruct((TOP_K, t), I32),
            jax.ShapeDtypeStruct((TOP_K, t), F32),
            jax.ShapeDtypeStruct((N_EXPERTS, LANES), F32),
        ],
        compiler_params=_params("arbitrary"),
        name="router",
    )(ha, hb, w_router_t, e_bias_col)


def _position_kernel(eid_ref, off_ref, dest_ref, run_ref):
    tm = eid_ref.shape[1]

    @pl.when(pl.program_id(0) == 0)
    def _():
        run_ref[...] = off_ref[...]

    eid = eid_ref[...]
    ioe = lax.broadcasted_iota(I32, (N_EXPERTS, tm), 0)
    picks = [ioe == eid[k:k + 1, :] for k in range(TOP_K)]
    chosen = functools.reduce(jnp.logical_or, picks)
    before = (lax.broadcasted_iota(I32, (tm, tm), 0) < lax.broadcasted_iota(I32, (tm, tm), 1))
    rank = jnp.dot(chosen.astype(BF16), before.astype(BF16), preferred_element_type=F32)
    pos = run_ref[...] + rank
    dest = [jnp.sum(jnp.where(pk, pos, 0.0), axis=0, keepdims=True) for pk in picks]
    dest_ref[...] = jnp.concatenate(dest, axis=0).astype(I32)
    run_ref[...] += jnp.sum(chosen.astype(F32), axis=1, keepdims=True)


def _positions(eid, off_col):
    t = eid.shape[1]
    tm = POSITION_TILE
    return pl.pallas_call(
        _position_kernel,
        grid=(t // tm,),
        in_specs=[pl.BlockSpec((TOP_K, tm), lambda i: (0, i)), _resident((N_EXPERTS, 1))],
        out_specs=pl.BlockSpec((TOP_K, tm), lambda i: (0, i)),
        out_shape=jax.ShapeDtypeStruct((TOP_K, t), I32),
        scratch_shapes=[pltpu.VMEM((N_EXPERTS, 1), F32)],
        compiler_params=_params("arbitrary"),
        name="positions",
    )(eid, off_col)


def _sc_mesh():
    return plsc.VectorSubcoreMesh(core_axis_name="core", subcore_axis_name="subcore")


def _dispatch_rows(x, dest, n_rows):
    t, width = x.shape
    n_picks = dest.shape[0]

    @functools.partial(pl.kernel, out_type=jax.ShapeDtypeStruct((n_rows, width), x.dtype),
                       mesh=_sc_mesh(), scratch_types=[])
    def scatter(x_hbm, i_hbm, o_hbm):
        def body(x_vmem, i_vmem):
            for k in range(n_picks):
                pltpu.sync_copy(x_vmem, o_hbm.at[i_vmem.at[k]])

        pltpu.emit_pipeline(
            body,
            grid=(t // SC_WINDOW,),
            in_specs=[pl.BlockSpec((SC_WINDOW, width), lambda i: (i, 0)),
                      pl.BlockSpec((n_picks, SC_WINDOW), lambda i: (0, i))],
            out_specs=[],
            core_axis_name=("core", "subcore"),
            dimension_semantics=(pltpu.PARALLEL,),
        )(x_hbm, i_hbm)

    return scatter(x, dest)


def _gather_rows(y, idx):
    n = idx.shape[1]
    width = y.shape[1]

    @functools.partial(pl.kernel, out_type=jax.ShapeDtypeStruct((n, width), y.dtype),
                       mesh=_sc_mesh(), scratch_types=[])
    def gather(y_hbm, i_hbm, o_hbm):
        def body(i_vmem, o_vmem):
            pltpu.sync_copy(y_hbm.at[i_vmem.at[0]], o_vmem)

        pltpu.emit_pipeline(
            body,
            grid=(n // SC_WINDOW,),
            in_specs=[pl.BlockSpec((1, SC_WINDOW), lambda i: (0, i))],
            out_specs=[pl.BlockSpec((SC_WINDOW, width), lambda i: (i, 0))],
            core_axis_name=("core", "subcore"),
            dimension_semantics=(pltpu.PARALLEL,),
        )(i_hbm, o_hbm)

    return gather(y, idx)


def _expert_kernel(te_ref, nvalid_ref, xa_ref, xb_ref, w1_ref, w3_ref, w2_ref, ya_ref, yb_ref,
                   w1_bf, w3_bf, w2_bf):
    g = pl.program_id(0)

    @pl.when(g < nvalid_ref[0])
    def _():
        @pl.when(jnp.logical_or(g == 0, te_ref[g] != te_ref[jnp.maximum(g - 1, 0)]))
        def _():
            w1_bf[...] = w1_ref[0].astype(BF16)
            w3_bf[...] = w3_ref[0].astype(BF16)
            w2_bf[...] = w2_ref[0].astype(BF16)

        hi, lo = _unpack_bf16_pairs(jnp.concatenate([xa_ref[...], xb_ref[...]], axis=1))
        hi = hi.astype(BF16)
        lo = lo.astype(BF16)

        def up(w_bf):
            return (jnp.dot(hi, w_bf[:PACKED, :], preferred_element_type=F32)
                    + jnp.dot(lo, w_bf[PACKED:, :], preferred_element_type=F32))

        act = _silu(up(w1_bf)) * up(w3_bf)
        words = _pack_bf16_pairs(_mm(act, w2_bf[...]))
        ya_ref[...] = words[:, :HALF]
        yb_ref[...] = words[:, HALF:]


def _experts(xa, xb, tile_expert, n_valid, w1, w3, w2):
    rows = xa.shape[0]
    d, f = D_MODEL, D_EXPERT
    tm = EXPERT_TILE
    rows_spec = lambda: pl.BlockSpec((tm, HALF), lambda g, te, nv: (jnp.minimum(g, nv[0] - 1), 0))
    return pl.pallas_call(
        _expert_kernel,
        grid_spec=pltpu.PrefetchScalarGridSpec(
            num_scalar_prefetch=2,
            grid=(rows // tm,),
            in_specs=[
                rows_spec(), rows_spec(),
                pl.BlockSpec((1, d, f), lambda g, te, nv: (te[g], 0, 0)),
                pl.BlockSpec((1, d, f), lambda g, te, nv: (te[g], 0, 0)),
                pl.BlockSpec((1, f, d), lambda g, te, nv: (te[g], 0, 0)),
            ],
            out_specs=[rows_spec(), rows_spec()],
            scratch_shapes=[pltpu.VMEM((d, f), BF16), pltpu.VMEM((d, f), BF16), pltpu.VMEM((f, d), BF16)],
        ),
        out_shape=[jax.ShapeDtypeStruct((rows, HALF), I32), jax.ShapeDtypeStruct((rows, HALF), I32)],
        compiler_params=_params("arbitrary"),
        name="experts",
    )(tile_expert, n_valid, xa, xb, w1, w3, w2)


def _final_kernel(ha_ref, hb_ref, x1_ref, ada_ref, gpost2_ref, w8_ref, yga_ref, ygb_ref, ws1_ref, ws3_ref,
                  ws2_ref, out_ref):
    hi, lo = _unpack_bf16_pairs(jnp.concatenate([ha_ref[...], hb_ref[...]], axis=1))
    hi = hi.astype(BF16)
    lo = lo.astype(BF16)

    def up(w_ref):
        return (jnp.dot(hi, w_ref[:PACKED, :], preferred_element_type=F32)
                + jnp.dot(lo, w_ref[PACKED:, :], preferred_element_type=F32))

    f = _mm(_silu(up(ws1_ref)) * up(ws3_ref), ws2_ref[...])
    for k in range(TOP_K):
        hi, lo = _unpack_bf16_pairs(jnp.concatenate([yga_ref[k], ygb_ref[k]], axis=1))
        f = f + w8_ref[:, k:k + 1] * jnp.concatenate([hi, lo], axis=1)
    out_ref[...] = x1_ref[...] + ada_ref[5, 0] * _rmsnorm(f, gpost2_ref[...])


def _final(ha, hb, x1, ada4, tokens_per_ada_block, w8_t, yga, ygb, first_token, p, tm):
    t, d = x1.shape
    f = D_EXPERT
    first_block = first_token // tm
    tok = lambda: pl.BlockSpec((tm, d), lambda i: (i, 0))
    if ada4.shape[2] == 1:
        ada_spec = pl.BlockSpec((6, 1, 1, d), lambda i: (0, i * tm // tokens_per_ada_block, 0, 0))
    else:
        ada_spec = pl.BlockSpec((6, 1, tm, d), lambda i: (0, 0, i, 0))
    gathered = lambda: pl.BlockSpec((TOP_K, tm, HALF), lambda i: (0, first_block + i, 0))
    return pl.pallas_call(
        _final_kernel,
        grid=(t // tm,),
        in_specs=[
            pl.BlockSpec((tm, HALF), lambda i: (first_block + i, 0)),
            pl.BlockSpec((tm, HALF), lambda i: (first_block + i, 0)),
            tok(), ada_spec, _resident((1, d)),
            pl.BlockSpec((tm, TOP_K), lambda i: (first_block + i, 0)),
            gathered(), gathered(),
            _resident((d, f)), _resident((d, f)), _resident((f, d)),
        ],
        out_specs=tok(),
        out_shape=jax.ShapeDtypeStruct((t, d), F32),
        compiler_params=_params("arbitrary"),
        name="final",
    )(ha, hb, x1, ada4, p["g_post2"], w8_t, yga, ygb, p["ws1"], p["ws3"], p["ws2"])


def kernel(x_prompt, x_sample, state_sconv, state_cconv, c_prompt, c_sample, w_ada, b_ada, g_pre1, g_post1,
           g_pre2, g_post2, w_in, w_sconv, w_out_a, w_cconv, b_cconv, ln_g, ln_b, w_out_b, b_out_b, w_o,
           w_router, e_bias, w1, w3, w2, ws1, ws3, ws2):
    assert w_ada.shape[0] == 1, "single-layer trunk"
    nb, seq, d = x_prompt.shape
    ns = x_sample.shape[0]
    n_prompt = nb * seq
    n_tokens = n_prompt + ns
    p = {
        "g_pre1": g_pre1, "g_post1": g_post1, "g_pre2": g_pre2, "g_post2": g_post2,
        "w_in": w_in[0].astype(BF16), "w_sconv": w_sconv[0], "w_out_a": w_out_a[0].astype(BF16),
        "w_cconv": w_cconv[0], "b_cconv": b_cconv, "ln_g": ln_g, "ln_b": ln_b,
        "w_out_b": w_out_b[0].astype(BF16), "b_out_b": b_out_b, "w_o": w_o[0].astype(BF16),
        "ws1": ws1[0].astype(BF16), "ws3": ws3[0].astype(BF16), "ws2": ws2[0].astype(BF16),
    }
    w_router_t = w_router[0].T.astype(BF16)
    e_bias_col = e_bias[0].reshape(N_EXPERTS, 1)

    ada = _ada(jnp.concatenate([c_prompt, c_sample], axis=0), w_ada[0], b_ada)
    ada = ada.reshape(nb + ns, 6, d).transpose(1, 0, 2)
    ada_p = ada[:, :nb].reshape(6, nb, 1, d)
    ada_s = ada[:, nb:].reshape(6, 1, ns, d)

    x1_p, ha_p, hb_p, u_tail, glu_tail = _prompt_mixer(x_prompt, ada_p, p)
    x1_s, ha_s, hb_s, u_s, glu_s = _sample_mixer(
        x_sample.reshape(ns, d), ada_s,
        state_sconv[0].reshape(ns, (K_SCONV - 1) * d), state_cconv[0].reshape(ns, (K_CCONV - 1) * d), p)
    ha_p = ha_p.reshape(n_prompt, HALF)
    hb_p = hb_p.reshape(n_prompt, HALF)

    eid_p, w8_p, cnt_p = _router(ha_p, hb_p, w_router_t, e_bias_col, ROUTER_TILE)
    eid_s, w8_s, cnt_s = _router(ha_s, hb_s, w_router_t, e_bias_col, ns)
    eid = jnp.concatenate([eid_p, eid_s], axis=1)
    w8_t = jnp.concatenate([w8_p, w8_s], axis=1).T
    counts = (cnt_p[:, 0] + cnt_s[:, 0]).astype(I32)
    padded = (counts + EXPERT_TILE - 1) // EXPERT_TILE * EXPERT_TILE
    ends = jnp.cumsum(padded)
    n_rows = n_tokens * TOP_K + N_EXPERTS * EXPERT_TILE
    tile_start = jnp.arange(n_rows // EXPERT_TILE, dtype=I32) * EXPERT_TILE
    tile_expert = jnp.minimum(jnp.sum(tile_start[:, None] >= ends[None, :], axis=1), N_EXPERTS - 1).astype(I32)
    n_valid = (ends[-1:] // EXPERT_TILE).astype(I32)
    dest = _positions(eid, (ends - padded).astype(F32).reshape(N_EXPERTS, 1))

    ha = jnp.concatenate([ha_p, ha_s], axis=0)
    hb = jnp.concatenate([hb_p, hb_s], axis=0)
    xa = _dispatch_rows(ha, dest, n_rows)
    xb = _dispatch_rows(hb, dest, n_rows)
    ya, yb = _experts(xa, xb, tile_expert, n_valid, w1[0], w3[0], w2[0])
    flat = dest.reshape(1, TOP_K * n_tokens)
    yga = _gather_rows(ya, flat).reshape(TOP_K, n_tokens, HALF)
    ygb = _gather_rows(yb, flat).reshape(TOP_K, n_tokens, HALF)

    y_p = _final(ha, hb, x1_p.reshape(n_prompt, d), ada_p, seq, w8_t, yga, ygb, 0, p, FINAL_TILE)
    y_s = _final(ha, hb, x1_s, ada_s, 1, w8_t, yga, ygb, n_prompt, p, ns)

    new_sconv_prompt = u_tail[:, SUBLANES - (K_SCONV - 1):][None]
    new_cconv_prompt = glu_tail[:, CONV_HALO - (K_CCONV - 1):][None]
    new_sconv_sample = jnp.concatenate([state_sconv[0][:, 1:], u_s[:, None]], axis=1)[None]
    new_cconv_sample = jnp.concatenate([state_cconv[0][:, 1:], glu_s[:, None]], axis=1)[None]
    return (y_p.reshape(nb, seq, d), y_s.reshape(ns, 1, d),
            new_sconv_prompt, new_sconv_sample, new_cconv_prompt, new_cconv_sample)
```

```python
import functools

import jax
import jax.numpy as jnp
from jax import lax
from jax.experimental import pallas as pl
from jax.experimental.pallas import tpu as pltpu
from jax.experimental.pallas import tpu_sc as plsc

D_MODEL = 1024
K_SCONV = 3
K_CCONV = 31
N_EXPERTS = 64
TOP_K = 8
N_GROUPS = 8
GROUP_SIZE = N_EXPERTS // N_GROUPS
TOPK_GROUPS = 4
D_EXPERT = 256
ROUTED_SCALE = 2.5
RMS_EPS = 1e-6
LN_EPS = 1e-5

F32 = jnp.float32
BF16 = jnp.bfloat16
I32 = jnp.int32
NEG_INF = float("-inf")

VMEM_LIMIT_BYTES_V7X = 56 * 1024 * 1024
SUBLANES = 8
LANES = 128

MIXER_TILE = 512
CONV_HALO = 32
CONV_ROW_CHUNK = 64
CONV_LANE_CHUNK = 128
PROJ_LANE_CHUNK = 256
ROUTER_TILE = 384
POSITION_TILE = 384
EXPERT_TILE = 1280
FINAL_TILE = 512
PACKED = D_MODEL // 2
HALF = PACKED // 2
SC_WINDOW = 128


def _params(*semantics, flags=None):
    return pltpu.CompilerParams(dimension_semantics=semantics, vmem_limit_bytes=VMEM_LIMIT_BYTES_V7X, flags=flags)


def _resident(shape):
    zeros = (0,) * len(shape)
    return pl.BlockSpec(shape, lambda *_: zeros, pipeline_mode=pl.Buffered(1))


def _silu(x):
    return x * jax.nn.sigmoid(x)


def _rmsnorm(x, g):
    return x * lax.rsqrt(jnp.mean(x * x, axis=-1, keepdims=True) + RMS_EPS) * g


def _layernorm(x, g, b):
    mu = jnp.mean(x, axis=-1, keepdims=True)
    xc = x - mu
    var = jnp.mean(xc * xc, axis=-1, keepdims=True)
    return xc * lax.rsqrt(var + LN_EPS) * g + b


def _mm(a, w):
    return jnp.dot(a.astype(BF16), w.astype(BF16), preferred_element_type=F32)


def _pack_bf16_pairs(v):
    n = v.shape[1] // 2
    bits = lax.bitcast_convert_type(v.astype(BF16).astype(F32), I32)
    return bits[:, :n] | lax.shift_right_logical(bits[:, n:], 16)


def _unpack_bf16_pairs(words):
    hi = lax.bitcast_convert_type(words & jnp.int32(-65536), F32)
    lo = lax.bitcast_convert_type(lax.shift_left(words, 16), F32)
    return hi, lo


def _ada_kernel(c_ref, w_ref, b_ref, o_ref):
    o_ref[...] = _mm(_silu(c_ref[...]), w_ref[...]) + b_ref[...]


def _ada(c, w_ada, b_ada):
    n = c.shape[0]
    width = w_ada.shape[1]
    bn = D_MODEL
    return pl.pallas_call(
        _ada_kernel,
        grid=(width // bn,),
        in_specs=[
            pl.BlockSpec((n, D_MODEL), lambda i: (0, 0)),
            pl.BlockSpec((D_MODEL, bn), lambda i: (0, i)),
            pl.BlockSpec((1, bn), lambda i: (0, i)),
        ],
        out_specs=pl.BlockSpec((n, bn), lambda i: (0, i)),
        out_shape=jax.ShapeDtypeStruct((n, width), F32),
        compiler_params=_params("arbitrary"),
        name="ada",
    )(c, w_ada, b_ada)


def _mixer_tail(x, ada_ref, y_a, y_b, gate_a, gate_b, w_o_ref, gpost1_ref, gpre2_ref, x1_ref, ha_ref, hb_ref):
    m = gate_a * y_a + gate_b * y_b
    t = _mm(m, w_o_ref[...])
    x1 = x + ada_ref[2, 0] * _rmsnorm(t, gpost1_ref[...])
    h2 = _rmsnorm(x1, gpre2_ref[...]) * (1.0 + ada_ref[4, 0]) + ada_ref[3, 0]
    x1_ref[...] = x1.reshape(x1_ref.shape)
    words = _pack_bf16_pairs(h2)
    ha_ref[...] = words[:, :HALF].reshape(ha_ref.shape)
    hb_ref[...] = words[:, HALF:].reshape(hb_ref.shape)


def _prompt_mixer_kernel(x_ref, ada_ref, gpre1_ref, gpost1_ref, gpre2_ref, w_in_ref, w_sconv_ref,
                         w_out_a_ref, w_cconv_ref, b_cconv_ref, ln_g_ref, ln_b_ref, w_out_b_ref,
                         b_out_b_ref, w_o_ref, ha_init_ref, hb_init_ref,
                         x1_ref, ha_ref, hb_ref, u_tail_ref, glu_tail_ref,
                         u_ext, glu_ext, cv_buf, h_buf, av_buf, ga_buf, gb_buf):
    del ha_init_ref, hb_init_ref
    tm = x_ref.shape[1]
    d = D_MODEL

    @pl.when(pl.program_id(1) == 0)
    def _():
        u_ext[pl.ds(0, SUBLANES), :] = jnp.zeros((SUBLANES, d), F32)
        glu_ext[pl.ds(0, CONV_HALO), :] = jnp.zeros((CONV_HALO, d), F32)

    h_buf[...] = (_rmsnorm(x_ref[0], gpre1_ref[...]) * (1.0 + ada_ref[1, 0]) + ada_ref[0, 0]).astype(BF16)

    def proj(i, lanes=slice(0, d)):
        return jnp.dot(h_buf[...], w_in_ref[:, i * d + lanes.start:i * d + lanes.stop],
                       preferred_element_type=F32)

    glu = proj(3) * jax.nn.sigmoid(proj(4))
    glu_ext[pl.ds(CONV_HALO, tm), :] = glu
    glu_tail_ref[0] = glu[tm - CONV_HALO:, :]
    first = CONV_HALO - (K_CCONV - 1)

    def conv_piece(r0, lanes):
        acc = None
        for res in range(SUBLANES):
            rows = CONV_ROW_CHUNK + (SUBLANES if res else 0)
            part = None
            for k in range(K_CCONV):
                if (first + k) % SUBLANES != res:
                    continue
                start = r0 + first + k - res
                term = w_cconv_ref[k:k + 1, lanes] * glu_ext[start:start + rows, lanes]
                part = term if part is None else part + term
            part = part[res:res + CONV_ROW_CHUNK, :]
            acc = part if acc is None else acc + part
        cv_buf[r0:r0 + CONV_ROW_CHUNK, lanes] = acc

    def branch_a(lanes):
        u = proj(2, lanes) * proj(0, lanes)
        u_ext[pl.ds(SUBLANES, tm), lanes] = u
        v = (w_sconv_ref[0:1, lanes] * u_ext[pl.ds(SUBLANES - 2, tm), lanes]
             + w_sconv_ref[1:2, lanes] * u_ext[pl.ds(SUBLANES - 1, tm), lanes]
             + w_sconv_ref[2:3, lanes] * u)
        u_tail_ref[0, :, lanes] = u[tm - SUBLANES:, :]
        u_ext[pl.ds(0, SUBLANES), lanes] = u[tm - SUBLANES:, :]
        av_buf[:, lanes] = (proj(1, lanes) * v).astype(BF16)

    def gates(lanes):
        ga_buf[:, lanes] = jax.nn.sigmoid(proj(5, lanes))
        gb_buf[:, lanes] = jax.nn.sigmoid(proj(6, lanes))

    for c in range(0, d, PROJ_LANE_CHUNK):
        lanes = slice(c, c + PROJ_LANE_CHUNK)

        @pl.when(pl.program_id(1) >= 0)
        def _(lanes=lanes):
            for r0 in range(0, tm, CONV_ROW_CHUNK):
                for l0 in range(lanes.start, lanes.stop, CONV_LANE_CHUNK):
                    conv_piece(r0, slice(l0, l0 + CONV_LANE_CHUNK))
            branch_a(lanes)
            gates(lanes)

    glu_ext[pl.ds(0, CONV_HALO), :] = glu_ext[pl.ds(tm, CONV_HALO), :]

    y_a = jnp.dot(av_buf[...], w_out_a_ref[...], preferred_element_type=F32)
    cv = cv_buf[...] + b_cconv_ref[...]
    y_b = _mm(_silu(_layernorm(cv, ln_g_ref[...], ln_b_ref[...])), w_out_b_ref[...]) + b_out_b_ref[...]

    _mixer_tail(x_ref[0], ada_ref, y_a, y_b, ga_buf[...], gb_buf[...], w_o_ref, gpost1_ref, gpre2_ref,
                x1_ref, ha_ref, hb_ref)


def _prompt_mixer(x, ada4, n_tokens, p):
    nb, seq, d = x.shape
    tm = MIXER_TILE
    row = lambda: _resident((1, d))
    tok = lambda w: pl.BlockSpec((1, tm, w), lambda b, j: (b, j, 0))
    packed = lambda: pl.BlockSpec((tm, HALF), lambda b, j: (b * (seq // tm) + j, 0))
    return pl.pallas_call(
        _prompt_mixer_kernel,
        grid=(nb, seq // tm),
        in_specs=[
            tok(d),
            pl.BlockSpec((6, 1, 1, d), lambda b, j: (0, b, 0, 0)),
            row(), row(), row(),
            _resident(p["w_in"].shape),
            _resident((K_SCONV, d)),
            _resident((d, d)),
            _resident((K_CCONV, d)),
            row(), row(), row(),
            _resident((d, d)),
            row(),
            _resident((d, d)),
            pl.BlockSpec(memory_space=pl.ANY), pl.BlockSpec(memory_space=pl.ANY),
        ],
        out_specs=[
            tok(d), packed(), packed(),
            pl.BlockSpec((1, SUBLANES, d), lambda b, j: (b, 0, 0)),
            pl.BlockSpec((1, CONV_HALO, d), lambda b, j: (b, 0, 0)),
        ],
        out_shape=[
            jax.ShapeDtypeStruct((nb, seq, d), F32),
            jax.ShapeDtypeStruct((n_tokens, HALF), I32),
            jax.ShapeDtypeStruct((n_tokens, HALF), I32),
            jax.ShapeDtypeStruct((nb, SUBLANES, d), F32),
            jax.ShapeDtypeStruct((nb, CONV_HALO, d), F32),
        ],
        scratch_shapes=[
            pltpu.VMEM((tm + SUBLANES, d), F32),
            pltpu.VMEM((tm + CONV_HALO + SUBLANES, d), F32),
            pltpu.VMEM((tm, d), F32),
            pltpu.VMEM((tm, d), BF16),
            pltpu.VMEM((tm, d), BF16),
            pltpu.VMEM((tm, d), F32),
            pltpu.VMEM((tm, d), F32),
        ],
        input_output_aliases={15: 1, 16: 2},
        compiler_params=_params("arbitrary", "arbitrary"),
        name="prompt_mixer",
    )(x, ada4, p["g_pre1"], p["g_post1"], p["g_pre2"], p["w_in"], p["w_sconv"], p["w_out_a"],
      p["w_cconv"], p["b_cconv"], p["ln_g"], p["ln_b"], p["w_out_b"], p["b_out_b"], p["w_o"],
      jnp.zeros((n_tokens, HALF), I32), jnp.zeros((n_tokens, HALF), I32))


def _sample_mixer_kernel(x_ref, ada_ref, st_a_ref, st_b_ref, gpre1_ref, gpost1_ref, gpre2_ref, w_in_ref,
                         w_sconv_ref, w_out_a_ref, w_cconv_ref, b_cconv_ref, ln_g_ref, ln_b_ref,
                         w_out_b_ref, b_out_b_ref, w_o_ref, ha_all_ref, hb_all_ref,
                         x1_ref, ha_ref, hb_ref, new_a_ref, new_b_ref):
    del ha_all_ref, hb_all_ref
    d = D_MODEL
    x = x_ref[...]
    h = (_rmsnorm(x, gpre1_ref[...]) * (1.0 + ada_ref[1, 0]) + ada_ref[0, 0]).astype(BF16)

    def proj(i):
        return jnp.dot(h, w_in_ref[:, i * d:(i + 1) * d], preferred_element_type=F32)

    u = proj(2) * proj(0)
    v = w_sconv_ref[K_SCONV - 1:K_SCONV, :] * u
    for k in range(K_SCONV - 1):
        v = v + w_sconv_ref[k:k + 1, :] * st_a_ref[:, k, :]
    new_a_ref[:, :K_SCONV - 2, :] = st_a_ref[:, 1:, :]
    new_a_ref[:, K_SCONV - 2, :] = u
    y_a = _mm(proj(1) * v, w_out_a_ref[...])

    glu = proj(3) * jax.nn.sigmoid(proj(4))
    cv = w_cconv_ref[K_CCONV - 1:K_CCONV, :] * glu + b_cconv_ref[...]
    for k in range(K_CCONV - 1):
        cv = cv + w_cconv_ref[k:k + 1, :] * st_b_ref[:, k, :]
    new_b_ref[:, :K_CCONV - 2, :] = st_b_ref[:, 1:, :]
    new_b_ref[:, K_CCONV - 2, :] = glu
    y_b = _mm(_silu(_layernorm(cv, ln_g_ref[...], ln_b_ref[...])), w_out_b_ref[...]) + b_out_b_ref[...]

    _mixer_tail(x, ada_ref, y_a, y_b, jax.nn.sigmoid(proj(5)), jax.nn.sigmoid(proj(6)), w_o_ref, gpost1_ref,
                gpre2_ref, x1_ref, ha_ref, hb_ref)


def _sample_mixer(x, ada4, st_a, st_b, ha_all, hb_all, first_token, p):
    n, d = x.shape
    tb = 32
    first_block = first_token // tb
    row = lambda: _resident((1, d))
    tok = lambda w: pl.BlockSpec((tb, w), lambda i: (i, 0))
    state = lambda k: pl.BlockSpec((tb, k - 1, d), lambda i: (i, 0, 0))
    packed = lambda: pl.BlockSpec((tb, HALF), lambda i: (first_block + i, 0))
    return pl.pallas_call(
        _sample_mixer_kernel,
        grid=(n // tb,),
        in_specs=[
            tok(d),
            pl.BlockSpec((6, 1, tb, d), lambda i: (0, 0, i, 0)),
            state(K_SCONV), state(K_CCONV),
            row(), row(), row(),
            _resident(p["w_in"].shape),
            _resident((K_SCONV, d)),
            _resident((d, d)),
            _resident((K_CCONV, d)),
            row(), row(), row(),
            _resident((d, d)),
            row(),
            _resident((d, d)),
            pl.BlockSpec(memory_space=pl.ANY), pl.BlockSpec(memory_space=pl.ANY),
        ],
        out_specs=[tok(d), packed(), packed(), state(K_SCONV), state(K_CCONV)],
        out_shape=[
            jax.ShapeDtypeStruct((n, d), F32),
            jax.ShapeDtypeStruct(ha_all.shape, I32),
            jax.ShapeDtypeStruct(hb_all.shape, I32),
            jax.ShapeDtypeStruct(st_a.shape, F32),
            jax.ShapeDtypeStruct(st_b.shape, F32),
        ],
        input_output_aliases={17: 1, 18: 2},
        compiler_params=_params("arbitrary"),
        name="sample_mixer",
    )(x, ada4, st_a, st_b, p["g_pre1"], p["g_post1"], p["g_pre2"], p["w_in"], p["w_sconv"], p["w_out_a"],
      p["w_cconv"], p["b_cconv"], p["ln_g"], p["ln_b"], p["w_out_b"], p["b_out_b"], p["w_o"], ha_all, hb_all)


def _first_index_of_max(work, iota, axis, limit):
    m = jnp.max(work, axis=axis, keepdims=True)
    return m, jnp.min(jnp.where(work == m, iota, limit), axis=axis, keepdims=True)


def _router_kernel(ha_ref, hb_ref, wr_ref, bias_ref, eid_ref, w8_ref, cnt_ref):
    tm = ha_ref.shape[0]
    hi, lo = _unpack_bf16_pairs(jnp.concatenate([ha_ref[...], hb_ref[...]], axis=1))
    nt = (((1,), (1,)), ((), ()))
    logits = (lax.dot_general(wr_ref[:, :PACKED], hi.astype(BF16), nt, preferred_element_type=F32)
              + lax.dot_general(wr_ref[:, PACKED:], lo.astype(BF16), nt, preferred_element_type=F32))
    scores = jax.nn.sigmoid(logits)
    sel = scores + bias_ref[...]

    sel3 = sel.reshape(N_GROUPS, GROUP_SIZE, tm)
    io3 = lax.broadcasted_iota(I32, sel3.shape, 1)
    m1, i1 = _first_index_of_max(sel3, io3, 1, GROUP_SIZE)
    m2 = jnp.max(jnp.where(io3 == i1, NEG_INF, sel3), axis=1, keepdims=True)
    gscore = (m1 + m2).reshape(N_GROUPS, tm)

    iog = lax.broadcasted_iota(I32, gscore.shape, 0)
    gmask = jnp.zeros(gscore.shape, jnp.bool_)
    for _ in range(TOPK_GROUPS):
        _, gi = _first_index_of_max(gscore, iog, 0, N_GROUPS)
        pick = iog == gi
        gmask = jnp.logical_or(gmask, pick)
        gscore = jnp.where(pick, NEG_INF, gscore)

    work = jnp.where(gmask.reshape(N_GROUPS, 1, tm), sel3, NEG_INF).reshape(N_EXPERTS, tm)
    ioe = lax.broadcasted_iota(I32, work.shape, 0)
    chosen = jnp.zeros(work.shape, jnp.bool_)
    eids, ws = [], []
    for _ in range(TOP_K):
        _, ei = _first_index_of_max(work, ioe, 0, N_EXPERTS)
        pick = ioe == ei
        eids.append(ei)
        ws.append(jnp.sum(jnp.where(pick, scores, 0.0), axis=0, keepdims=True))
        chosen = jnp.logical_or(chosen, pick)
        work = jnp.where(pick, NEG_INF, work)

    w = jnp.concatenate(ws, axis=0)
    eid_ref[...] = jnp.concatenate(eids, axis=0)
    w8_ref[...] = w / jnp.sum(w, axis=0, keepdims=True) * ROUTED_SCALE

    @pl.when(pl.program_id(0) == 0)
    def _():
        cnt_ref[...] = jnp.zeros(cnt_ref.shape, F32)

    cnt_ref[...] += jnp.sum(chosen.astype(F32), axis=1, keepdims=True)


def _router(ha, hb, w_router_t, e_bias_col, tm):
    t = ha.shape[0]
    d = D_MODEL
    picks = lambda: pl.BlockSpec((TOP_K, tm), lambda i: (0, i))
    half_rows = lambda: pl.BlockSpec((tm, HALF), lambda i: (i, 0))
    return pl.pallas_call(
        _router_kernel,
        grid=(t // tm,),
        in_specs=[
            half_rows(), half_rows(),
            _resident((N_EXPERTS, d)),
            _resident((N_EXPERTS, 1)),
        ],
        out_specs=[picks(), picks(), pl.BlockSpec((N_EXPERTS, LANES), lambda i: (0, 0))],
        out_shape=[
            jax.ShapeDtypeStruct((TOP_K, t), I32),
            jax.ShapeDtypeStruct((TOP_K, t), F32),
            jax.ShapeDtypeStruct((N_EXPERTS, LANES), F32),
        ],
        compiler_params=_params("arbitrary"),
        name="router",
    )(ha, hb, w_router_t, e_bias_col)


def _position_kernel(eid_ref, off_ref, dest_ref, run_ref):
    tm = eid_ref.shape[1]

    @pl.when(pl.program_id(0) == 0)
    def _():
        run_ref[...] = off_ref[...]

    eid = eid_ref[...]
    ioe = lax.broadcasted_iota(I32, (N_EXPERTS, tm), 0)
    picks = [ioe == eid[k:k + 1, :] for k in range(TOP_K)]
    chosen = functools.reduce(jnp.logical_or, picks)
    before = (lax.broadcasted_iota(I32, (tm, tm), 0) < lax.broadcasted_iota(I32, (tm, tm), 1))
    rank = jnp.dot(chosen.astype(BF16), before.astype(BF16), preferred_element_type=F32)
    pos = run_ref[...] + rank
    dest = [jnp.sum(jnp.where(pk, pos, 0.0), axis=0, keepdims=True) for pk in picks]
    dest_ref[...] = jnp.concatenate(dest, axis=0).astype(I32)
    run_ref[...] += jnp.sum(chosen.astype(F32), axis=1, keepdims=True)


def _positions(eid, off_col):
    t = eid.shape[1]
    tm = POSITION_TILE
    return pl.pallas_call(
        _position_kernel,
        grid=(t // tm,),
        in_specs=[pl.BlockSpec((TOP_K, tm), lambda i: (0, i)), _resident((N_EXPERTS, 1))],
        out_specs=pl.BlockSpec((TOP_K, tm), lambda i: (0, i)),
        out_shape=jax.ShapeDtypeStruct((TOP_K, t), I32),
        scratch_shapes=[pltpu.VMEM((N_EXPERTS, 1), F32)],
        compiler_params=_params("arbitrary"),
        name="positions",
    )(eid, off_col)


def _sc_mesh():
    return plsc.VectorSubcoreMesh(core_axis_name="core", subcore_axis_name="subcore")


def _dispatch_rows(x, dest, n_rows):
    t, width = x.shape
    n_picks = dest.shape[0]

    @functools.partial(pl.kernel, out_type=jax.ShapeDtypeStruct((n_rows, width), x.dtype),
                       mesh=_sc_mesh(), scratch_types=[])
    def scatter(x_hbm, i_hbm, o_hbm):
        def body(x_vmem, i_vmem):
            for k in range(n_picks):
                pltpu.sync_copy(x_vmem, o_hbm.at[i_vmem.at[k]])

        pltpu.emit_pipeline(
            body,
            grid=(t // SC_WINDOW,),
            in_specs=[pl.BlockSpec((SC_WINDOW, width), lambda i: (i, 0)),
                      pl.BlockSpec((n_picks, SC_WINDOW), lambda i: (0, i))],
            out_specs=[],
            core_axis_name=("core", "subcore"),
            dimension_semantics=(pltpu.PARALLEL,),
        )(x_hbm, i_hbm)

    return scatter(x, dest)


def _gather_rows(y, idx):
    n = idx.shape[1]
    width = y.shape[1]

    @functools.partial(pl.kernel, out_type=jax.ShapeDtypeStruct((n, width), y.dtype),
                       mesh=_sc_mesh(), scratch_types=[])
    def gather(y_hbm, i_hbm, o_hbm):
        def body(i_vmem, o_vmem):
            pltpu.sync_copy(y_hbm.at[i_vmem.at[0]], o_vmem)

        pltpu.emit_pipeline(
            body,
            grid=(n // SC_WINDOW,),
            in_specs=[pl.BlockSpec((1, SC_WINDOW), lambda i: (0, i))],
            out_specs=[pl.BlockSpec((SC_WINDOW, width), lambda i: (i, 0))],
            core_axis_name=("core", "subcore"),
            dimension_semantics=(pltpu.PARALLEL,),
        )(i_hbm, o_hbm)

    return gather(y, idx)


def _expert_kernel(first_tile_ref, n_tiles_ref, total_ref, xa_hbm, xb_hbm, w1_ref, w3_ref, w2_ref,
                   ya_hbm, yb_hbm, xa_buf, xb_buf, ya_buf, yb_buf, in_sem, out_sem, w1_bf, w3_bf, w2_bf):
    e = pl.program_id(0)
    tm = EXPERT_TILE
    total = total_ref[0]
    first_tile = first_tile_ref[e]
    halves_in = ((xa_hbm, xa_buf), (xb_hbm, xb_buf))
    halves_out = ((ya_buf, ya_hbm), (yb_buf, yb_hbm))

    def rows(g):
        return pl.ds(pl.multiple_of(g * tm, tm), tm)

    def in_copies(g, slot):
        return [pltpu.make_async_copy(hbm.at[rows(g)], buf.at[slot], in_sem.at[i, slot])
                for i, (hbm, buf) in enumerate(halves_in)]

    def out_copies(g, slot):
        return [pltpu.make_async_copy(buf.at[slot], hbm.at[rows(g)], out_sem.at[i, slot])
                for i, (buf, hbm) in enumerate(halves_out)]

    @pl.when(n_tiles_ref[e] > 0)
    def _():
        w1_bf[...] = w1_ref[0].astype(BF16)
        w3_bf[...] = w3_ref[0].astype(BF16)
        w2_bf[...] = w2_ref[0].astype(BF16)

    def tile(t, carry):
        g = first_tile + t
        slot = g % 2

        @pl.when(g == 0)
        def _():
            for c in in_copies(g, slot):
                c.start()

        @pl.when(g + 1 < total)
        def _():
            for c in in_copies(g + 1, 1 - slot):
                c.start()

        for c in in_copies(g, slot):
            c.wait()

        @pl.when(g >= 2)
        def _():
            for c in out_copies(g - 2, slot):
                c.wait()

        hi, lo = _unpack_bf16_pairs(jnp.concatenate([xa_buf[slot], xb_buf[slot]], axis=1))
        hi = hi.astype(BF16)
        lo = lo.astype(BF16)

        def up(w_bf):
            return (jnp.dot(hi, w_bf[:PACKED, :], preferred_element_type=F32)
                    + jnp.dot(lo, w_bf[PACKED:, :], preferred_element_type=F32))

        act = _silu(up(w1_bf)) * up(w3_bf)
        words = _pack_bf16_pairs(_mm(act, w2_bf[...]))
        ya_buf[slot] = words[:, :HALF]
        yb_buf[slot] = words[:, HALF:]
        for c in out_copies(g, slot):
            c.start()
        return carry

    lax.fori_loop(0, n_tiles_ref[e], tile, 0)

    @pl.when(e == pl.num_programs(0) - 1)
    def _():
        for back in (2, 1):
            @pl.when(total >= back)
            def _(back=back):
                for c in out_copies(total - back, (total - back) % 2):
                    c.wait()


def _experts(xa, xb, first_tile, n_tiles, total_tiles, w1, w3, w2):
    rows = xa.shape[0]
    d, f = D_MODEL, D_EXPERT
    tm = EXPERT_TILE
    weights = lambda a, b: pl.BlockSpec((1, a, b), lambda e, *_: (e, 0, 0))
    ring = lambda: pltpu.VMEM((2, tm, HALF), I32)
    return pl.pallas_call(
        _expert_kernel,
        grid_spec=pltpu.PrefetchScalarGridSpec(
            num_scalar_prefetch=3,
            grid=(N_EXPERTS,),
            in_specs=[
                pl.BlockSpec(memory_space=pl.ANY), pl.BlockSpec(memory_space=pl.ANY),
                weights(d, f), weights(d, f), weights(f, d),
            ],
            out_specs=[pl.BlockSpec(memory_space=pl.ANY), pl.BlockSpec(memory_space=pl.ANY)],
            scratch_shapes=[
                ring(), ring(), ring(), ring(),
                pltpu.SemaphoreType.DMA((2, 2)), pltpu.SemaphoreType.DMA((2, 2)),
                pltpu.VMEM((d, f), BF16), pltpu.VMEM((d, f), BF16), pltpu.VMEM((f, d), BF16),
            ],
        ),
        out_shape=[jax.ShapeDtypeStruct((rows, HALF), I32), jax.ShapeDtypeStruct((rows, HALF), I32)],
        compiler_params=_params("arbitrary"),
        name="experts",
    )(first_tile, n_tiles, total_tiles, xa, xb, w1, w3, w2)


def _final_kernel(ha_ref, hb_ref, x1_ref, ada_ref, gpost2_ref, w8_ref, yga_ref, ygb_ref, ws1_ref, ws3_ref,
                  ws2_ref, out_ref):
    hi, lo = _unpack_bf16_pairs(jnp.concatenate([ha_ref[...], hb_ref[...]], axis=1))
    hi = hi.astype(BF16)
    lo = lo.astype(BF16)

    def up(w_ref):
        return (jnp.dot(hi, w_ref[:PACKED, :], preferred_element_type=F32)
                + jnp.dot(lo, w_ref[PACKED:, :], preferred_element_type=F32))

    f = _mm(_silu(up(ws1_ref)) * up(ws3_ref), ws2_ref[...])
    for k in range(TOP_K):
        hi, lo = _unpack_bf16_pairs(jnp.concatenate([yga_ref[k], ygb_ref[k]], axis=1))
        f = f + w8_ref[:, k:k + 1] * jnp.concatenate([hi, lo], axis=1)
    out_ref[...] = x1_ref[...] + ada_ref[5, 0] * _rmsnorm(f, gpost2_ref[...])


def _final(ha, hb, x1, ada4, tokens_per_ada_block, w8_t, yga, ygb, first_token, p, tm):
    t, d = x1.shape
    f = D_EXPERT
    first_block = first_token // tm
    tok = lambda: pl.BlockSpec((tm, d), lambda i: (i, 0))
    if ada4.shape[2] == 1:
        ada_spec = pl.BlockSpec((6, 1, 1, d), lambda i: (0, i * tm // tokens_per_ada_block, 0, 0))
    else:
        ada_spec = pl.BlockSpec((6, 1, tm, d), lambda i: (0, 0, i, 0))
    gathered = lambda: pl.BlockSpec((TOP_K, tm, HALF), lambda i: (0, first_block + i, 0))
    return pl.pallas_call(
        _final_kernel,
        grid=(t // tm,),
        in_specs=[
            pl.BlockSpec((tm, HALF), lambda i: (first_block + i, 0)),
            pl.BlockSpec((tm, HALF), lambda i: (first_block + i, 0)),
            tok(), ada_spec, _resident((1, d)),
            pl.BlockSpec((tm, TOP_K), lambda i: (first_block + i, 0)),
            gathered(), gathered(),
            _resident((d, f)), _resident((d, f)), _resident((f, d)),
        ],
        out_specs=tok(),
        out_shape=jax.ShapeDtypeStruct((t, d), F32),
        compiler_params=_params("arbitrary"),
        name="final",
    )(ha, hb, x1, ada4, p["g_post2"], w8_t, yga, ygb, p["ws1"], p["ws3"], p["ws2"])


def kernel(x_prompt, x_sample, state_sconv, state_cconv, c_prompt, c_sample, w_ada, b_ada, g_pre1, g_post1,
           g_pre2, g_post2, w_in, w_sconv, w_out_a, w_cconv, b_cconv, ln_g, ln_b, w_out_b, b_out_b, w_o,
           w_router, e_bias, w1, w3, w2, ws1, ws3, ws2):
    assert w_ada.shape[0] == 1, "single-layer trunk"
    nb, seq, d = x_prompt.shape
    ns = x_sample.shape[0]
    n_prompt = nb * seq
    n_tokens = n_prompt + ns
    p = {
        "g_pre1": g_pre1, "g_post1": g_post1, "g_pre2": g_pre2, "g_post2": g_post2,
        "w_in": w_in[0].astype(BF16), "w_sconv": w_sconv[0], "w_out_a": w_out_a[0].astype(BF16),
        "w_cconv": w_cconv[0], "b_cconv": b_cconv, "ln_g": ln_g, "ln_b": ln_b,
        "w_out_b": w_out_b[0].astype(BF16), "b_out_b": b_out_b, "w_o": w_o[0].astype(BF16),
        "ws1": ws1[0].astype(BF16), "ws3": ws3[0].astype(BF16), "ws2": ws2[0].astype(BF16),
    }
    w_router_t = w_router[0].T.astype(BF16)
    e_bias_col = e_bias[0].reshape(N_EXPERTS, 1)

    ada = _ada(jnp.concatenate([c_prompt, c_sample], axis=0), w_ada[0], b_ada)
    ada = ada.reshape(nb + ns, 6, d).transpose(1, 0, 2)
    ada_p = ada[:, :nb].reshape(6, nb, 1, d)
    ada_s = ada[:, nb:].reshape(6, 1, ns, d)

    x1_p, ha, hb, u_tail, glu_tail = _prompt_mixer(x_prompt, ada_p, n_tokens, p)
    x1_s, ha, hb, new_sconv_sample, new_cconv_sample = _sample_mixer(
        x_sample.reshape(ns, d), ada_s, state_sconv[0], state_cconv[0], ha, hb, n_prompt, p)

    eid, w8, cnt = _router(ha, hb, w_router_t, e_bias_col, ROUTER_TILE)
    w8_t = w8.T
    counts = cnt[:, 0].astype(I32)
    n_tiles = (counts + EXPERT_TILE - 1) // EXPERT_TILE
    tile_ends = jnp.cumsum(n_tiles)
    first_tile = tile_ends - n_tiles
    n_rows = n_tokens * TOP_K + N_EXPERTS * EXPERT_TILE
    dest = _positions(eid, (first_tile * EXPERT_TILE).astype(F32).reshape(N_EXPERTS, 1))

    xa = _dispatch_rows(ha, dest, n_rows)
    xb = _dispatch_rows(hb, dest, n_rows)
    ya, yb = _experts(xa, xb, first_tile, n_tiles, tile_ends[-1:], w1[0], w3[0], w2[0])
    flat = dest.reshape(1, TOP_K * n_tokens)
    yga = _gather_rows(ya, flat).reshape(TOP_K, n_tokens, HALF)
    ygb = _gather_rows(yb, flat).reshape(TOP_K, n_tokens, HALF)

    y_p = _final(ha, hb, x1_p.reshape(n_prompt, d), ada_p, seq, w8_t, yga, ygb, 0, p, FINAL_TILE)
    y_s = _final(ha, hb, x1_s, ada_s, 1, w8_t, yga, ygb, n_prompt, p, ns)

    new_sconv_prompt = u_tail[:, SUBLANES - (K_SCONV - 1):][None]
    new_cconv_prompt = glu_tail[:, CONV_HALO - (K_CCONV - 1):][None]
    return (y_p.reshape(nb, seq, d), y_s.reshape(ns, 1, d),
            new_sconv_prompt, new_sconv_sample[None], new_cconv_prompt, new_cconv_sample[None])
```

```python
import functools

import jax
import jax.numpy as jnp
from jax import lax
from jax.experimental import pallas as pl
from jax.experimental.pallas import tpu as pltpu
from jax.experimental.pallas import tpu_sc as plsc

D_MODEL = 1024
K_SCONV = 3
K_CCONV = 31
N_EXPERTS = 64
TOP_K = 8
N_GROUPS = 8
GROUP_SIZE = N_EXPERTS // N_GROUPS
TOPK_GROUPS = 4
D_EXPERT = 256
ROUTED_SCALE = 2.5
RMS_EPS = 1e-6
LN_EPS = 1e-5

F32 = jnp.float32
BF16 = jnp.bfloat16
I32 = jnp.int32
NEG_INF = float("-inf")

VMEM_LIMIT_BYTES_V7X = 56 * 1024 * 1024
SUBLANES = 8
LANES = 128

MIXER_TILE = 512
CONV_HALO = 32
CONV_ROW_CHUNK = 64
CONV_LANE_CHUNK = 128
PROJ_LANE_CHUNK = 256
ROUTER_TILE = 384
POSITION_TILE = 384
EXPERT_TILE = 768
EXPERT_COPY_ROWS = 256
EXPERT_RING = 3
FINAL_TILE = 512
PACKED = D_MODEL // 2
HALF = PACKED // 2
SC_WINDOW = 128


def _params(*semantics, flags=None):
    return pltpu.CompilerParams(dimension_semantics=semantics, vmem_limit_bytes=VMEM_LIMIT_BYTES_V7X, flags=flags)


def _resident(shape):
    zeros = (0,) * len(shape)
    return pl.BlockSpec(shape, lambda *_: zeros, pipeline_mode=pl.Buffered(1))


def _silu(x):
    return x * jax.nn.sigmoid(x)


def _rmsnorm(x, g):
    return x * lax.rsqrt(jnp.mean(x * x, axis=-1, keepdims=True) + RMS_EPS) * g


def _layernorm(x, g, b):
    mu = jnp.mean(x, axis=-1, keepdims=True)
    xc = x - mu
    var = jnp.mean(xc * xc, axis=-1, keepdims=True)
    return xc * lax.rsqrt(var + LN_EPS) * g + b


def _mm(a, w):
    return jnp.dot(a.astype(BF16), w.astype(BF16), preferred_element_type=F32)


def _pack_bf16_pairs(v):
    n = v.shape[1] // 2
    bits = lax.bitcast_convert_type(v.astype(BF16).astype(F32), I32)
    return bits[:, :n] | lax.shift_right_logical(bits[:, n:], 16)


def _unpack_bf16_pairs(words):
    hi = lax.bitcast_convert_type(words & jnp.int32(-65536), F32)
    lo = lax.bitcast_convert_type(lax.shift_left(words, 16), F32)
    return hi, lo


def _ada_kernel(c_ref, w_ref, b_ref, o_ref):
    o_ref[...] = _mm(_silu(c_ref[...]), w_ref[...]) + b_ref[...]


def _ada(c, w_ada, b_ada):
    n = c.shape[0]
    width = w_ada.shape[1]
    bn = D_MODEL
    return pl.pallas_call(
        _ada_kernel,
        grid=(width // bn,),
        in_specs=[
            pl.BlockSpec((n, D_MODEL), lambda i: (0, 0)),
            pl.BlockSpec((D_MODEL, bn), lambda i: (0, i)),
            pl.BlockSpec((1, bn), lambda i: (0, i)),
        ],
        out_specs=pl.BlockSpec((n, bn), lambda i: (0, i)),
        out_shape=jax.ShapeDtypeStruct((n, width), F32),
        compiler_params=_params("arbitrary"),
        name="ada",
    )(c, w_ada, b_ada)


def _mixer_tail(x, ada_ref, y_a, y_b, gate_a, gate_b, w_o_ref, gpost1_ref, gpre2_ref, x1_ref, ha_ref, hb_ref):
    m = gate_a * y_a + gate_b * y_b
    t = _mm(m, w_o_ref[...])
    x1 = x + ada_ref[2, 0] * _rmsnorm(t, gpost1_ref[...])
    h2 = _rmsnorm(x1, gpre2_ref[...]) * (1.0 + ada_ref[4, 0]) + ada_ref[3, 0]
    x1_ref[...] = x1.reshape(x1_ref.shape)
    words = _pack_bf16_pairs(h2)
    ha_ref[...] = words[:, :HALF].reshape(ha_ref.shape)
    hb_ref[...] = words[:, HALF:].reshape(hb_ref.shape)


def _prompt_mixer_kernel(x_ref, ada_ref, gpre1_ref, gpost1_ref, gpre2_ref, w_in_ref, w_sconv_ref,
                         w_out_a_ref, w_cconv_ref, b_cconv_ref, ln_g_ref, ln_b_ref, w_out_b_ref,
                         b_out_b_ref, w_o_ref, ha_init_ref, hb_init_ref,
                         x1_ref, ha_ref, hb_ref, u_tail_ref, glu_tail_ref,
                         u_ext, glu_ext, cv_buf, h_buf, av_buf, ga_buf, gb_buf):
    del ha_init_ref, hb_init_ref
    tm = x_ref.shape[1]
    d = D_MODEL

    @pl.when(pl.program_id(1) == 0)
    def _():
        u_ext[pl.ds(0, SUBLANES), :] = jnp.zeros((SUBLANES, d), F32)
        glu_ext[pl.ds(0, CONV_HALO), :] = jnp.zeros((CONV_HALO, d), F32)

    h_buf[...] = (_rmsnorm(x_ref[0], gpre1_ref[...]) * (1.0 + ada_ref[1, 0]) + ada_ref[0, 0]).astype(BF16)

    def proj(i, lanes=slice(0, d)):
        return jnp.dot(h_buf[...], w_in_ref[:, i * d + lanes.start:i * d + lanes.stop],
                       preferred_element_type=F32)

    glu = proj(3) * jax.nn.sigmoid(proj(4))
    glu_ext[pl.ds(CONV_HALO, tm), :] = glu
    glu_tail_ref[0] = glu[tm - CONV_HALO:, :]
    first = CONV_HALO - (K_CCONV - 1)

    def conv_piece(r0, lanes):
        acc = None
        for res in range(SUBLANES):
            rows = CONV_ROW_CHUNK + (SUBLANES if res else 0)
            part = None
            for k in range(K_CCONV):
                if (first + k) % SUBLANES != res:
                    continue
                start = r0 + first + k - res
                term = w_cconv_ref[k:k + 1, lanes] * glu_ext[start:start + rows, lanes]
                part = term if part is None else part + term
            part = part[res:res + CONV_ROW_CHUNK, :]
            acc = part if acc is None else acc + part
        cv_buf[r0:r0 + CONV_ROW_CHUNK, lanes] = acc

    def branch_a(lanes):
        u = proj(2, lanes) * proj(0, lanes)
        u_ext[pl.ds(SUBLANES, tm), lanes] = u
        v = (w_sconv_ref[0:1, lanes] * u_ext[pl.ds(SUBLANES - 2, tm), lanes]
             + w_sconv_ref[1:2, lanes] * u_ext[pl.ds(SUBLANES - 1, tm), lanes]
             + w_sconv_ref[2:3, lanes] * u)
        u_tail_ref[0, :, lanes] = u[tm - SUBLANES:, :]
        u_ext[pl.ds(0, SUBLANES), lanes] = u[tm - SUBLANES:, :]
        av_buf[:, lanes] = (proj(1, lanes) * v).astype(BF16)

    def gates(lanes):
        ga_buf[:, lanes] = jax.nn.sigmoid(proj(5, lanes))
        gb_buf[:, lanes] = jax.nn.sigmoid(proj(6, lanes))

    for c in range(0, d, PROJ_LANE_CHUNK):
        lanes = slice(c, c + PROJ_LANE_CHUNK)

        @pl.when(pl.program_id(1) >= 0)
        def _(lanes=lanes):
            for r0 in range(0, tm, CONV_ROW_CHUNK):
                for l0 in range(lanes.start, lanes.stop, CONV_LANE_CHUNK):
                    conv_piece(r0, slice(l0, l0 + CONV_LANE_CHUNK))
            branch_a(lanes)
            gates(lanes)

    glu_ext[pl.ds(0, CONV_HALO), :] = glu_ext[pl.ds(tm, CONV_HALO), :]

    y_a = jnp.dot(av_buf[...], w_out_a_ref[...], preferred_element_type=F32)
    cv = cv_buf[...] + b_cconv_ref[...]
    y_b = _mm(_silu(_layernorm(cv, ln_g_ref[...], ln_b_ref[...])), w_out_b_ref[...]) + b_out_b_ref[...]

    _mixer_tail(x_ref[0], ada_ref, y_a, y_b, ga_buf[...], gb_buf[...], w_o_ref, gpost1_ref, gpre2_ref,
                x1_ref, ha_ref, hb_ref)


def _prompt_mixer(x, ada4, n_tokens, p):
    nb, seq, d = x.shape
    tm = MIXER_TILE
    row = lambda: _resident((1, d))
    tok = lambda w: pl.BlockSpec((1, tm, w), lambda b, j: (b, j, 0))
    packed = lambda: pl.BlockSpec((tm, HALF), lambda b, j: (b * (seq // tm) + j, 0))
    return pl.pallas_call(
        _prompt_mixer_kernel,
        grid=(nb, seq // tm),
        in_specs=[
            tok(d),
            pl.BlockSpec((6, 1, 1, d), lambda b, j: (0, b, 0, 0)),
            row(), row(), row(),
            _resident(p["w_in"].shape),
            _resident((K_SCONV, d)),
            _resident((d, d)),
            _resident((K_CCONV, d)),
            row(), row(), row(),
            _resident((d, d)),
            row(),
            _resident((d, d)),
            pl.BlockSpec(memory_space=pl.ANY), pl.BlockSpec(memory_space=pl.ANY),
        ],
        out_specs=[
            tok(d), packed(), packed(),
            pl.BlockSpec((1, SUBLANES, d), lambda b, j: (b, 0, 0)),
            pl.BlockSpec((1, CONV_HALO, d), lambda b, j: (b, 0, 0)),
        ],
        out_shape=[
            jax.ShapeDtypeStruct((nb, seq, d), F32),
            jax.ShapeDtypeStruct((n_tokens, HALF), I32),
            jax.ShapeDtypeStruct((n_tokens, HALF), I32),
            jax.ShapeDtypeStruct((nb, SUBLANES, d), F32),
            jax.ShapeDtypeStruct((nb, CONV_HALO, d), F32),
        ],
        scratch_shapes=[
            pltpu.VMEM((tm + SUBLANES, d), F32),
            pltpu.VMEM((tm + CONV_HALO + SUBLANES, d), F32),
            pltpu.VMEM((tm, d), F32),
            pltpu.VMEM((tm, d), BF16),
            pltpu.VMEM((tm, d), BF16),
            pltpu.VMEM((tm, d), F32),
            pltpu.VMEM((tm, d), F32),
        ],
        input_output_aliases={15: 1, 16: 2},
        compiler_params=_params("arbitrary", "arbitrary"),
        name="prompt_mixer",
    )(x, ada4, p["g_pre1"], p["g_post1"], p["g_pre2"], p["w_in"], p["w_sconv"], p["w_out_a"],
      p["w_cconv"], p["b_cconv"], p["ln_g"], p["ln_b"], p["w_out_b"], p["b_out_b"], p["w_o"],
      jnp.zeros((n_tokens, HALF), I32), jnp.zeros((n_tokens, HALF), I32))


def _sample_mixer_kernel(x_ref, ada_ref, st_a_ref, st_b_ref, gpre1_ref, gpost1_ref, gpre2_ref, w_in_ref,
                         w_sconv_ref, w_out_a_ref, w_cconv_ref, b_cconv_ref, ln_g_ref, ln_b_ref,
                         w_out_b_ref, b_out_b_ref, w_o_ref, ha_all_ref, hb_all_ref,
                         x1_ref, ha_ref, hb_ref, new_a_ref, new_b_ref):
    del ha_all_ref, hb_all_ref
    d = D_MODEL
    x = x_ref[...]
    h = (_rmsnorm(x, gpre1_ref[...]) * (1.0 + ada_ref[1, 0]) + ada_ref[0, 0]).astype(BF16)

    def proj(i):
        return jnp.dot(h, w_in_ref[:, i * d:(i + 1) * d], preferred_element_type=F32)

    u = proj(2) * proj(0)
    v = w_sconv_ref[K_SCONV - 1:K_SCONV, :] * u
    for k in range(K_SCONV - 1):
        v = v + w_sconv_ref[k:k + 1, :] * st_a_ref[:, k, :]
    new_a_ref[:, :K_SCONV - 2, :] = st_a_ref[:, 1:, :]
    new_a_ref[:, K_SCONV - 2, :] = u
    y_a = _mm(proj(1) * v, w_out_a_ref[...])

    glu = proj(3) * jax.nn.sigmoid(proj(4))
    cv = w_cconv_ref[K_CCONV - 1:K_CCONV, :] * glu + b_cconv_ref[...]
    for k in range(K_CCONV - 1):
        cv = cv + w_cconv_ref[k:k + 1, :] * st_b_ref[:, k, :]
    new_b_ref[:, :K_CCONV - 2, :] = st_b_ref[:, 1:, :]
    new_b_ref[:, K_CCONV - 2, :] = glu
    y_b = _mm(_silu(_layernorm(cv, ln_g_ref[...], ln_b_ref[...])), w_out_b_ref[...]) + b_out_b_ref[...]

    _mixer_tail(x, ada_ref, y_a, y_b, jax.nn.sigmoid(proj(5)), jax.nn.sigmoid(proj(6)), w_o_ref, gpost1_ref,
                gpre2_ref, x1_ref, ha_ref, hb_ref)


def _sample_mixer(x, ada4, st_a, st_b, ha_all, hb_all, first_token, p):
    n, d = x.shape
    tb = 32
    first_block = first_token // tb
    row = lambda: _resident((1, d))
    tok = lambda w: pl.BlockSpec((tb, w), lambda i: (i, 0))
    state = lambda k: pl.BlockSpec((tb, k - 1, d), lambda i: (i, 0, 0))
    packed = lambda: pl.BlockSpec((tb, HALF), lambda i: (first_block + i, 0))
    return pl.pallas_call(
        _sample_mixer_kernel,
        grid=(n // tb,),
        in_specs=[
            tok(d),
            pl.BlockSpec((6, 1, tb, d), lambda i: (0, 0, i, 0)),
            state(K_SCONV), state(K_CCONV),
            row(), row(), row(),
            _resident(p["w_in"].shape),
            _resident((K_SCONV, d)),
            _resident((d, d)),
            _resident((K_CCONV, d)),
            row(), row(), row(),
            _resident((d, d)),
            row(),
            _resident((d, d)),
            pl.BlockSpec(memory_space=pl.ANY), pl.BlockSpec(memory_space=pl.ANY),
        ],
        out_specs=[tok(d), packed(), packed(), state(K_SCONV), state(K_CCONV)],
        out_shape=[
            jax.ShapeDtypeStruct((n, d), F32),
            jax.ShapeDtypeStruct(ha_all.shape, I32),
            jax.ShapeDtypeStruct(hb_all.shape, I32),
            jax.ShapeDtypeStruct(st_a.shape, F32),
            jax.ShapeDtypeStruct(st_b.shape, F32),
        ],
        input_output_aliases={17: 1, 18: 2},
        compiler_params=_params("arbitrary"),
        name="sample_mixer",
    )(x, ada4, st_a, st_b, p["g_pre1"], p["g_post1"], p["g_pre2"], p["w_in"], p["w_sconv"], p["w_out_a"],
      p["w_cconv"], p["b_cconv"], p["ln_g"], p["ln_b"], p["w_out_b"], p["b_out_b"], p["w_o"], ha_all, hb_all)


def _first_index_of_max(work, iota, axis, limit):
    m = jnp.max(work, axis=axis, keepdims=True)
    return m, jnp.min(jnp.where(work == m, iota, limit), axis=axis, keepdims=True)


def _router_kernel(ha_ref, hb_ref, wr_ref, bias_ref, eid_ref, w8_ref, cnt_ref):
    tm = ha_ref.shape[0]
    hi, lo = _unpack_bf16_pairs(jnp.concatenate([ha_ref[...], hb_ref[...]], axis=1))
    nt = (((1,), (1,)), ((), ()))
    logits = (lax.dot_general(wr_ref[:, :PACKED], hi.astype(BF16), nt, preferred_element_type=F32)
              + lax.dot_general(wr_ref[:, PACKED:], lo.astype(BF16), nt, preferred_element_type=F32))
    scores = jax.nn.sigmoid(logits)
    sel = scores + bias_ref[...]

    sel3 = sel.reshape(N_GROUPS, GROUP_SIZE, tm)
    io3 = lax.broadcasted_iota(I32, sel3.shape, 1)
    m1, i1 = _first_index_of_max(sel3, io3, 1, GROUP_SIZE)
    m2 = jnp.max(jnp.where(io3 == i1, NEG_INF, sel3), axis=1, keepdims=True)
    gscore = (m1 + m2).reshape(N_GROUPS, tm)

    iog = lax.broadcasted_iota(I32, gscore.shape, 0)
    gmask = jnp.zeros(gscore.shape, jnp.bool_)
    for _ in range(TOPK_GROUPS):
        _, gi = _first_index_of_max(gscore, iog, 0, N_GROUPS)
        pick = iog == gi
        gmask = jnp.logical_or(gmask, pick)
        gscore = jnp.where(pick, NEG_INF, gscore)

    work = jnp.where(gmask.reshape(N_GROUPS, 1, tm), sel3, NEG_INF).reshape(N_EXPERTS, tm)
    ioe = lax.broadcasted_iota(I32, work.shape, 0)
    chosen = jnp.zeros(work.shape, jnp.bool_)
    eids, ws = [], []
    for _ in range(TOP_K):
        _, ei = _first_index_of_max(work, ioe, 0, N_EXPERTS)
        pick = ioe == ei
        eids.append(ei)
        ws.append(jnp.sum(jnp.where(pick, scores, 0.0), axis=0, keepdims=True))
        chosen = jnp.logical_or(chosen, pick)
        work = jnp.where(pick, NEG_INF, work)

    w = jnp.concatenate(ws, axis=0)
    eid_ref[...] = jnp.concatenate(eids, axis=0)
    w8_ref[...] = w / jnp.sum(w, axis=0, keepdims=True) * ROUTED_SCALE

    @pl.when(pl.program_id(0) == 0)
    def _():
        cnt_ref[...] = jnp.zeros(cnt_ref.shape, F32)

    cnt_ref[...] += jnp.sum(chosen.astype(F32), axis=1, keepdims=True)


def _router(ha, hb, w_router_t, e_bias_col, tm):
    t = ha.shape[0]
    d = D_MODEL
    picks = lambda: pl.BlockSpec((TOP_K, tm), lambda i: (0, i))
    half_rows = lambda: pl.BlockSpec((tm, HALF), lambda i: (i, 0))
    return pl.pallas_call(
        _router_kernel,
        grid=(t // tm,),
        in_specs=[
            half_rows(), half_rows(),
            _resident((N_EXPERTS, d)),
            _resident((N_EXPERTS, 1)),
        ],
        out_specs=[picks(), picks(), pl.BlockSpec((N_EXPERTS, LANES), lambda i: (0, 0))],
        out_shape=[
            jax.ShapeDtypeStruct((TOP_K, t), I32),
            jax.ShapeDtypeStruct((TOP_K, t), F32),
            jax.ShapeDtypeStruct((N_EXPERTS, LANES), F32),
        ],
        compiler_params=_params("arbitrary"),
        name="router",
    )(ha, hb, w_router_t, e_bias_col)


def _position_kernel(eid_ref, off_ref, dest_ref, run_ref):
    tm = eid_ref.shape[1]

    @pl.when(pl.program_id(0) == 0)
    def _():
        run_ref[...] = off_ref[...]

    eid = eid_ref[...]
    ioe = lax.broadcasted_iota(I32, (N_EXPERTS, tm), 0)
    picks = [ioe == eid[k:k + 1, :] for k in range(TOP_K)]
    chosen = functools.reduce(jnp.logical_or, picks)
    before = (lax.broadcasted_iota(I32, (tm, tm), 0) < lax.broadcasted_iota(I32, (tm, tm), 1))
    rank = jnp.dot(chosen.astype(BF16), before.astype(BF16), preferred_element_type=F32)
    pos = run_ref[...] + rank
    dest = [jnp.sum(jnp.where(pk, pos, 0.0), axis=0, keepdims=True) for pk in picks]
    dest_ref[...] = jnp.concatenate(dest, axis=0).astype(I32)
    run_ref[...] += jnp.sum(chosen.astype(F32), axis=1, keepdims=True)


def _positions(eid, off_col):
    t = eid.shape[1]
    tm = POSITION_TILE
    return pl.pallas_call(
        _position_kernel,
        grid=(t // tm,),
        in_specs=[pl.BlockSpec((TOP_K, tm), lambda i: (0, i)), _resident((N_EXPERTS, 1))],
        out_specs=pl.BlockSpec((TOP_K, tm), lambda i: (0, i)),
        out_shape=jax.ShapeDtypeStruct((TOP_K, t), I32),
        scratch_shapes=[pltpu.VMEM((N_EXPERTS, 1), F32)],
        compiler_params=_params("arbitrary"),
        name="positions",
    )(eid, off_col)


def _sc_mesh():
    return plsc.VectorSubcoreMesh(core_axis_name="core", subcore_axis_name="subcore")


def _dispatch_rows(xa, xb, dest, n_rows):
    t, width = xa.shape
    n_picks = dest.shape[0]
    out = jax.ShapeDtypeStruct((n_rows, width), xa.dtype)

    @functools.partial(pl.kernel, out_type=[out, out], mesh=_sc_mesh(), scratch_types=[])
    def scatter(xa_hbm, xb_hbm, i_hbm, oa_hbm, ob_hbm):
        for x_hbm, o_hbm in ((xa_hbm, oa_hbm), (xb_hbm, ob_hbm)):
            def body(x_vmem, i_vmem, o_hbm=o_hbm):
                for k in range(n_picks):
                    pltpu.sync_copy(x_vmem, o_hbm.at[i_vmem.at[k]])

            pltpu.emit_pipeline(
                body,
                grid=(t // SC_WINDOW,),
                in_specs=[pl.BlockSpec((SC_WINDOW, width), lambda i: (i, 0)),
                          pl.BlockSpec((n_picks, SC_WINDOW), lambda i: (0, i))],
                out_specs=[],
                core_axis_name=("core", "subcore"),
                dimension_semantics=(pltpu.PARALLEL,),
            )(x_hbm, i_hbm)

    return scatter(xa, xb, dest)


def _gather_rows(ya, yb, idx):
    n = idx.shape[1]
    width = ya.shape[1]
    out = jax.ShapeDtypeStruct((n, width), ya.dtype)

    @functools.partial(pl.kernel, out_type=[out, out], mesh=_sc_mesh(), scratch_types=[])
    def gather(ya_hbm, yb_hbm, i_hbm, oa_hbm, ob_hbm):
        for y_hbm, o_hbm in ((ya_hbm, oa_hbm), (yb_hbm, ob_hbm)):
            def body(i_vmem, o_vmem, y_hbm=y_hbm):
                pltpu.sync_copy(y_hbm.at[i_vmem.at[0]], o_vmem)

            pltpu.emit_pipeline(
                body,
                grid=(n // SC_WINDOW,),
                in_specs=[pl.BlockSpec((1, SC_WINDOW), lambda i: (0, i))],
                out_specs=[pl.BlockSpec((SC_WINDOW, width), lambda i: (i, 0))],
                core_axis_name=("core", "subcore"),
                dimension_semantics=(pltpu.PARALLEL,),
            )(i_hbm, o_hbm)

    return gather(ya, yb, idx)


def _expert_kernel(first_tile_ref, n_tiles_ref, total_ref, tile_rows_ref, xa_hbm, xb_hbm, w1_ref, w3_ref, w2_ref,
                   ya_hbm, yb_hbm, xa_buf, xb_buf, ya_buf, yb_buf, in_sem, out_sem, w1_bf, w3_bf, w2_bf):
    e = pl.program_id(0)
    tm = EXPERT_TILE
    ring = EXPERT_RING
    total = total_ref[0]
    first_tile = first_tile_ref[e]

    def pieces(g, slot, bufs_hbm, sems, to_vmem):
        out = []
        for r0 in range(0, tm, EXPERT_COPY_ROWS):
            hbm_rows = pl.ds(pl.multiple_of(g * tm + r0, EXPERT_COPY_ROWS), EXPERT_COPY_ROWS)
            for i, (buf, hbm) in enumerate(bufs_hbm):
                vmem = buf.at[slot, pl.ds(r0, EXPERT_COPY_ROWS)]
                src, dst = (hbm.at[hbm_rows], vmem) if to_vmem else (vmem, hbm.at[hbm_rows])
                out.append((r0, pltpu.make_async_copy(src, dst, sems.at[i, slot])))
        return out

    def for_live_pieces(g, bufs_hbm, sems, to_vmem, act):
        slot = g % ring
        live_rows = tile_rows_ref[g]
        for r0, copy in pieces(g, slot, bufs_hbm, sems, to_vmem):
            pl.when(r0 < live_rows)(functools.partial(act, copy))

    ins = ((xa_buf, xa_hbm), (xb_buf, xb_hbm))
    outs = ((ya_buf, ya_hbm), (yb_buf, yb_hbm))
    start_in = lambda g: for_live_pieces(g, ins, in_sem, True, lambda c: c.start())
    wait_in = lambda g: for_live_pieces(g, ins, in_sem, True, lambda c: c.wait())
    start_out = lambda g: for_live_pieces(g, outs, out_sem, False, lambda c: c.start())
    wait_out = lambda g: for_live_pieces(g, outs, out_sem, False, lambda c: c.wait())

    @pl.when(n_tiles_ref[e] > 0)
    def _():
        w1_bf[...] = w1_ref[0].astype(BF16)
        w3_bf[...] = w3_ref[0].astype(BF16)
        w2_bf[...] = w2_ref[0].astype(BF16)

    def tile(t, carry):
        g = first_tile + t
        slot = g % ring

        @pl.when(g == 0)
        def _():
            for ahead in range(ring - 1):
                pl.when(ahead < total)(functools.partial(start_in, ahead))

        pl.when(g + ring - 1 < total)(functools.partial(start_in, g + ring - 1))
        wait_in(g)
        pl.when(g >= ring)(functools.partial(wait_out, g - ring))

        hi, lo = _unpack_bf16_pairs(jnp.concatenate([xa_buf[slot], xb_buf[slot]], axis=1))
        hi = hi.astype(BF16)
        lo = lo.astype(BF16)

        def up(w_bf):
            return (jnp.dot(hi, w_bf[:PACKED, :], preferred_element_type=F32)
                    + jnp.dot(lo, w_bf[PACKED:, :], preferred_element_type=F32))

        act = _silu(up(w1_bf)) * up(w3_bf)
        words = _pack_bf16_pairs(_mm(act, w2_bf[...]))
        ya_buf[slot] = words[:, :HALF]
        yb_buf[slot] = words[:, HALF:]
        start_out(g)
        return carry

    lax.fori_loop(0, n_tiles_ref[e], tile, 0)

    @pl.when(e == pl.num_programs(0) - 1)
    def _():
        for back in range(ring, 0, -1):
            pl.when(total >= back)(functools.partial(wait_out, total - back))


def _experts(xa, xb, first_tile, n_tiles, total_tiles, tile_rows, w1, w3, w2):
    rows = xa.shape[0]
    d, f = D_MODEL, D_EXPERT
    tm = EXPERT_TILE
    weights = lambda a, b: pl.BlockSpec((1, a, b), lambda e, *_: (e, 0, 0))
    ring = lambda: pltpu.VMEM((EXPERT_RING, tm, HALF), I32)
    sems = lambda: pltpu.SemaphoreType.DMA((2, EXPERT_RING))
    return pl.pallas_call(
        _expert_kernel,
        grid_spec=pltpu.PrefetchScalarGridSpec(
            num_scalar_prefetch=4,
            grid=(N_EXPERTS,),
            in_specs=[
                pl.BlockSpec(memory_space=pl.ANY), pl.BlockSpec(memory_space=pl.ANY),
                weights(d, f), weights(d, f), weights(f, d),
            ],
            out_specs=[pl.BlockSpec(memory_space=pl.ANY), pl.BlockSpec(memory_space=pl.ANY)],
            scratch_shapes=[
                ring(), ring(), ring(), ring(), sems(), sems(),
                pltpu.VMEM((d, f), BF16), pltpu.VMEM((d, f), BF16), pltpu.VMEM((f, d), BF16),
            ],
        ),
        out_shape=[jax.ShapeDtypeStruct((rows, HALF), I32), jax.ShapeDtypeStruct((rows, HALF), I32)],
        compiler_params=_params("arbitrary"),
        name="experts",
    )(first_tile, n_tiles, total_tiles, tile_rows, xa, xb, w1, w3, w2)


def _final_kernel(ha_ref, hb_ref, x1_ref, ada_ref, gpost2_ref, w8_ref, yga_ref, ygb_ref, ws1_ref, ws3_ref,
                  ws2_ref, out_ref):
    hi, lo = _unpack_bf16_pairs(jnp.concatenate([ha_ref[...], hb_ref[...]], axis=1))
    hi = hi.astype(BF16)
    lo = lo.astype(BF16)

    def up(w_ref):
        return (jnp.dot(hi, w_ref[:PACKED, :], preferred_element_type=F32)
                + jnp.dot(lo, w_ref[PACKED:, :], preferred_element_type=F32))

    f = _mm(_silu(up(ws1_ref)) * up(ws3_ref), ws2_ref[...])
    for k in range(TOP_K):
        hi, lo = _unpack_bf16_pairs(jnp.concatenate([yga_ref[k], ygb_ref[k]], axis=1))
        f = f + w8_ref[:, k:k + 1] * jnp.concatenate([hi, lo], axis=1)
    out_ref[...] = x1_ref[...] + ada_ref[5, 0] * _rmsnorm(f, gpost2_ref[...])


def _final(ha, hb, x1, ada4, tokens_per_ada_block, w8_t, yga, ygb, first_token, p, tm):
    t, d = x1.shape
    f = D_EXPERT
    first_block = first_token // tm
    tok = lambda: pl.BlockSpec((tm, d), lambda i: (i, 0))
    if ada4.shape[2] == 1:
        ada_spec = pl.BlockSpec((6, 1, 1, d), lambda i: (0, i * tm // tokens_per_ada_block, 0, 0))
    else:
        ada_spec = pl.BlockSpec((6, 1, tm, d), lambda i: (0, 0, i, 0))
    gathered = lambda: pl.BlockSpec((TOP_K, tm, HALF), lambda i: (0, first_block + i, 0))
    return pl.pallas_call(
        _final_kernel,
        grid=(t // tm,),
        in_specs=[
            pl.BlockSpec((tm, HALF), lambda i: (first_block + i, 0)),
            pl.BlockSpec((tm, HALF), lambda i: (first_block + i, 0)),
            tok(), ada_spec, _resident((1, d)),
            pl.BlockSpec((tm, TOP_K), lambda i: (first_block + i, 0)),
            gathered(), gathered(),
            _resident((d, f)), _resident((d, f)), _resident((f, d)),
        ],
        out_specs=tok(),
        out_shape=jax.ShapeDtypeStruct((t, d), F32),
        compiler_params=_params("arbitrary"),
        name="final",
    )(ha, hb, x1, ada4, p["g_post2"], w8_t, yga, ygb, p["ws1"], p["ws3"], p["ws2"])


def kernel(x_prompt, x_sample, state_sconv, state_cconv, c_prompt, c_sample, w_ada, b_ada, g_pre1, g_post1,
           g_pre2, g_post2, w_in, w_sconv, w_out_a, w_cconv, b_cconv, ln_g, ln_b, w_out_b, b_out_b, w_o,
           w_router, e_bias, w1, w3, w2, ws1, ws3, ws2):
    assert w_ada.shape[0] == 1, "single-layer trunk"
    nb, seq, d = x_prompt.shape
    ns = x_sample.shape[0]
    n_prompt = nb * seq
    n_tokens = n_prompt + ns
    p = {
        "g_pre1": g_pre1, "g_post1": g_post1, "g_pre2": g_pre2, "g_post2": g_post2,
        "w_in": w_in[0].astype(BF16), "w_sconv": w_sconv[0], "w_out_a": w_out_a[0].astype(BF16),
        "w_cconv": w_cconv[0], "b_cconv": b_cconv, "ln_g": ln_g, "ln_b": ln_b,
        "w_out_b": w_out_b[0].astype(BF16), "b_out_b": b_out_b, "w_o": w_o[0].astype(BF16),
        "ws1": ws1[0].astype(BF16), "ws3": ws3[0].astype(BF16), "ws2": ws2[0].astype(BF16),
    }
    w_router_t = w_router[0].T.astype(BF16)
    e_bias_col = e_bias[0].reshape(N_EXPERTS, 1)

    ada = _ada(jnp.concatenate([c_prompt, c_sample], axis=0), w_ada[0], b_ada)
    ada = ada.reshape(nb + ns, 6, d).transpose(1, 0, 2)
    ada_p = ada[:, :nb].reshape(6, nb, 1, d)
    ada_s = ada[:, nb:].reshape(6, 1, ns, d)

    x1_p, ha, hb, u_tail, glu_tail = _prompt_mixer(x_prompt, ada_p, n_tokens, p)
    x1_s, ha, hb, new_sconv_sample, new_cconv_sample = _sample_mixer(
        x_sample.reshape(ns, d), ada_s, state_sconv[0], state_cconv[0], ha, hb, n_prompt, p)

    eid, w8, cnt = _router(ha, hb, w_router_t, e_bias_col, ROUTER_TILE)
    w8_t = w8.T
    counts = cnt[:, 0].astype(I32)
    n_tiles = (counts + EXPERT_TILE - 1) // EXPERT_TILE
    tile_ends = jnp.cumsum(n_tiles)
    first_tile = tile_ends - n_tiles
    n_rows = n_tokens * TOP_K + N_EXPERTS * EXPERT_TILE
    dest = _positions(eid, (first_tile * EXPERT_TILE).astype(F32).reshape(N_EXPERTS, 1))
    tile_ids = jnp.arange(pl.cdiv(n_rows, EXPERT_TILE), dtype=I32)
    tile_owner = jnp.minimum(jnp.sum(tile_ids[:, None] >= tile_ends[None, :], axis=1), N_EXPERTS - 1)
    tile_rows = jnp.clip(counts[tile_owner] - (tile_ids - first_tile[tile_owner]) * EXPERT_TILE, 0, EXPERT_TILE)
    tile_rows = jnp.where(tile_ids < tile_ends[-1], tile_rows, 0).astype(I32)

    xa, xb = _dispatch_rows(ha, hb, dest, n_rows)
    ya, yb = _experts(xa, xb, first_tile, n_tiles, tile_ends[-1:], tile_rows, w1[0], w3[0], w2[0])
    flat = dest.reshape(1, TOP_K * n_tokens)
    yga, ygb = (g.reshape(TOP_K, n_tokens, HALF) for g in _gather_rows(ya, yb, flat))

    y_p = _final(ha, hb, x1_p.reshape(n_prompt, d), ada_p, seq, w8_t, yga, ygb, 0, p, FINAL_TILE)
    y_s = _final(ha, hb, x1_s, ada_s, 1, w8_t, yga, ygb, n_prompt, p, ns)

    new_sconv_prompt = u_tail[:, SUBLANES - (K_SCONV - 1):][None]
    new_cconv_prompt = glu_tail[:, CONV_HALO - (K_CCONV - 1):][None]
    return (y_p.reshape(nb, seq, d), y_s.reshape(ns, 1, d),
            new_sconv_prompt, new_sconv_sample[None], new_cconv_prompt, new_cconv_sample[None])
```

```python
import functools

import jax
import jax.numpy as jnp
from jax import lax
from jax.experimental import pallas as pl
from jax.experimental.pallas import tpu as pltpu
from jax.experimental.pallas import tpu_sc as plsc

D_MODEL = 1024
K_SCONV = 3
K_CCONV = 31
N_EXPERTS = 64
TOP_K = 8
N_GROUPS = 8
GROUP_SIZE = N_EXPERTS // N_GROUPS
TOPK_GROUPS = 4
D_EXPERT = 256
ROUTED_SCALE = 2.5
RMS_EPS = 1e-6
LN_EPS = 1e-5

F32 = jnp.float32
BF16 = jnp.bfloat16
I32 = jnp.int32
NEG_INF = float("-inf")

VMEM_LIMIT_BYTES_V7X = 56 * 1024 * 1024
SUBLANES = 8
LANES = 128

MIXER_TILE = 512
CONV_HALO = 32
CONV_ROW_CHUNK = 64
CONV_LANE_CHUNK = 128
PROJ_LANE_CHUNK = 256
ROUTER_TILE = 384
POSITION_TILE = 384
EXPERT_TILE = 768
EXPERT_COPY_ROWS = 256
EXPERT_RING = 3
FINAL_TILE = 512
PACKED = D_MODEL // 2
HALF = PACKED // 2
SC_WINDOW = 128


def _params(*semantics, flags=None):
    return pltpu.CompilerParams(dimension_semantics=semantics, vmem_limit_bytes=VMEM_LIMIT_BYTES_V7X, flags=flags)


def _resident(shape):
    zeros = (0,) * len(shape)
    return pl.BlockSpec(shape, lambda *_: zeros, pipeline_mode=pl.Buffered(1))


def _sigmoid(x):
    return 0.5 * jnp.tanh(0.5 * x) + 0.5


def _silu(x):
    return x * _sigmoid(x)


def _rmsnorm(x, g):
    return x * lax.rsqrt(jnp.mean(x * x, axis=-1, keepdims=True) + RMS_EPS) * g


def _layernorm(x, g, b):
    mu = jnp.mean(x, axis=-1, keepdims=True)
    xc = x - mu
    var = jnp.mean(xc * xc, axis=-1, keepdims=True)
    return xc * lax.rsqrt(var + LN_EPS) * g + b


def _mm(a, w):
    return jnp.dot(a.astype(BF16), w.astype(BF16), preferred_element_type=F32)


def _pack_bf16_pairs(v):
    n = v.shape[1] // 2
    bits = lax.bitcast_convert_type(v.astype(BF16).astype(F32), I32)
    return bits[:, :n] | lax.shift_right_logical(bits[:, n:], 16)


def _unpack_bf16_pairs(words):
    hi = lax.bitcast_convert_type(words & jnp.int32(-65536), F32)
    lo = lax.bitcast_convert_type(lax.shift_left(words, 16), F32)
    return hi, lo


def _ada_kernel(c_ref, w_ref, b_ref, o_ref):
    o_ref[...] = _mm(_silu(c_ref[...]), w_ref[...]) + b_ref[...]


def _ada(c, w_ada, b_ada):
    n = c.shape[0]
    width = w_ada.shape[1]
    bn = D_MODEL
    return pl.pallas_call(
        _ada_kernel,
        grid=(width // bn,),
        in_specs=[
            pl.BlockSpec((n, D_MODEL), lambda i: (0, 0)),
            pl.BlockSpec((D_MODEL, bn), lambda i: (0, i)),
            pl.BlockSpec((1, bn), lambda i: (0, i)),
        ],
        out_specs=pl.BlockSpec((n, bn), lambda i: (0, i)),
        out_shape=jax.ShapeDtypeStruct((n, width), F32),
        compiler_params=_params("arbitrary"),
        name="ada",
    )(c, w_ada, b_ada)


def _mixer_tail(x, ada_ref, y_a, y_b, gate_a, gate_b, w_o_ref, gpost1_ref, gpre2_ref, x1_ref, ha_ref, hb_ref):
    m = gate_a * y_a + gate_b * y_b
    t = _mm(m, w_o_ref[...])
    x1 = x + ada_ref[2, 0] * _rmsnorm(t, gpost1_ref[...])
    h2 = _rmsnorm(x1, gpre2_ref[...]) * (1.0 + ada_ref[4, 0]) + ada_ref[3, 0]
    x1_ref[...] = x1.reshape(x1_ref.shape)
    words = _pack_bf16_pairs(h2)
    ha_ref[...] = words[:, :HALF].reshape(ha_ref.shape)
    hb_ref[...] = words[:, HALF:].reshape(hb_ref.shape)


def _prompt_mixer_kernel(x_ref, ada_ref, gpre1_ref, gpost1_ref, gpre2_ref, w_in_ref, w_sconv_ref,
                         w_out_a_ref, w_cconv_ref, b_cconv_ref, ln_g_ref, ln_b_ref, w_out_b_ref,
                         b_out_b_ref, w_o_ref, ha_init_ref, hb_init_ref,
                         x1_ref, ha_ref, hb_ref, u_tail_ref, glu_tail_ref,
                         u_ext, glu_ext, cv_buf, h_buf, av_buf, ga_buf, gb_buf):
    del ha_init_ref, hb_init_ref
    tm = x_ref.shape[1]
    d = D_MODEL

    @pl.when(pl.program_id(1) == 0)
    def _():
        u_ext[pl.ds(0, SUBLANES), :] = jnp.zeros((SUBLANES, d), F32)
        glu_ext[pl.ds(0, CONV_HALO), :] = jnp.zeros((CONV_HALO, d), F32)

    h_buf[...] = (_rmsnorm(x_ref[0], gpre1_ref[...]) * (1.0 + ada_ref[1, 0]) + ada_ref[0, 0]).astype(BF16)

    def proj(i, lanes=slice(0, d)):
        return jnp.dot(h_buf[...], w_in_ref[:, i * d + lanes.start:i * d + lanes.stop],
                       preferred_element_type=F32)

    glu = proj(3) * _sigmoid(proj(4))
    glu_ext[pl.ds(CONV_HALO, tm), :] = glu
    glu_tail_ref[0] = glu[tm - CONV_HALO:, :]
    first = CONV_HALO - (K_CCONV - 1)

    def conv_piece(r0, lanes):
        acc = None
        for res in range(SUBLANES):
            rows = CONV_ROW_CHUNK + (SUBLANES if res else 0)
            part = None
            for k in range(K_CCONV):
                if (first + k) % SUBLANES != res:
                    continue
                start = r0 + first + k - res
                term = w_cconv_ref[k:k + 1, lanes] * glu_ext[start:start + rows, lanes]
                part = term if part is None else part + term
            part = part[res:res + CONV_ROW_CHUNK, :]
            acc = part if acc is None else acc + part
        cv_buf[r0:r0 + CONV_ROW_CHUNK, lanes] = acc

    def branch_a(lanes):
        u = proj(2, lanes) * proj(0, lanes)
        u_ext[pl.ds(SUBLANES, tm), lanes] = u
        v = (w_sconv_ref[0:1, lanes] * u_ext[pl.ds(SUBLANES - 2, tm), lanes]
             + w_sconv_ref[1:2, lanes] * u_ext[pl.ds(SUBLANES - 1, tm), lanes]
             + w_sconv_ref[2:3, lanes] * u)
        u_tail_ref[0, :, lanes] = u[tm - SUBLANES:, :]
        u_ext[pl.ds(0, SUBLANES), lanes] = u[tm - SUBLANES:, :]
        av_buf[:, lanes] = (proj(1, lanes) * v).astype(BF16)

    def gates(lanes):
        ga_buf[:, lanes] = _sigmoid(proj(5, lanes))
        gb_buf[:, lanes] = _sigmoid(proj(6, lanes))

    for c in range(0, d, PROJ_LANE_CHUNK):
        lanes = slice(c, c + PROJ_LANE_CHUNK)

        @pl.when(pl.program_id(1) + c >= 0)
        def _(lanes=lanes):
            for r0 in range(0, tm, CONV_ROW_CHUNK):
                for l0 in range(lanes.start, lanes.stop, CONV_LANE_CHUNK):
                    conv_piece(r0, slice(l0, l0 + CONV_LANE_CHUNK))
            branch_a(lanes)
            gates(lanes)

    glu_ext[pl.ds(0, CONV_HALO), :] = glu_ext[pl.ds(tm, CONV_HALO), :]

    y_a = jnp.dot(av_buf[...], w_out_a_ref[...], preferred_element_type=F32)
    cv = cv_buf[...] + b_cconv_ref[...]
    y_b = _mm(_silu(_layernorm(cv, ln_g_ref[...], ln_b_ref[...])), w_out_b_ref[...]) + b_out_b_ref[...]

    _mixer_tail(x_ref[0], ada_ref, y_a, y_b, ga_buf[...], gb_buf[...], w_o_ref, gpost1_ref, gpre2_ref,
                x1_ref, ha_ref, hb_ref)


def _prompt_mixer(x, ada4, n_tokens, p):
    nb, seq, d = x.shape
    tm = MIXER_TILE
    row = lambda: _resident((1, d))
    tok = lambda w: pl.BlockSpec((1, tm, w), lambda b, j: (b, j, 0))
    packed = lambda: pl.BlockSpec((tm, HALF), lambda b, j: (b * (seq // tm) + j, 0))
    return pl.pallas_call(
        _prompt_mixer_kernel,
        grid=(nb, seq // tm),
        in_specs=[
            tok(d),
            pl.BlockSpec((6, 1, 1, d), lambda b, j: (0, b, 0, 0)),
            row(), row(), row(),
            _resident(p["w_in"].shape),
            _resident((K_SCONV, d)),
            _resident((d, d)),
            _resident((K_CCONV, d)),
            row(), row(), row(),
            _resident((d, d)),
            row(),
            _resident((d, d)),
            pl.BlockSpec(memory_space=pl.ANY), pl.BlockSpec(memory_space=pl.ANY),
        ],
        out_specs=[
            tok(d), packed(), packed(),
            pl.BlockSpec((1, SUBLANES, d), lambda b, j: (b, 0, 0)),
            pl.BlockSpec((1, CONV_HALO, d), lambda b, j: (b, 0, 0)),
        ],
        out_shape=[
            jax.ShapeDtypeStruct((nb, seq, d), F32),
            jax.ShapeDtypeStruct((n_tokens, HALF), I32),
            jax.ShapeDtypeStruct((n_tokens, HALF), I32),
            jax.ShapeDtypeStruct((nb, SUBLANES, d), F32),
            jax.ShapeDtypeStruct((nb, CONV_HALO, d), F32),
        ],
        scratch_shapes=[
            pltpu.VMEM((tm + SUBLANES, d), F32),
            pltpu.VMEM((tm + CONV_HALO + SUBLANES, d), F32),
            pltpu.VMEM((tm, d), F32),
            pltpu.VMEM((tm, d), BF16),
            pltpu.VMEM((tm, d), BF16),
            pltpu.VMEM((tm, d), F32),
            pltpu.VMEM((tm, d), F32),
        ],
        input_output_aliases={15: 1, 16: 2},
        compiler_params=_params("arbitrary", "arbitrary"),
        name="prompt_mixer",
    )(x, ada4, p["g_pre1"], p["g_post1"], p["g_pre2"], p["w_in"], p["w_sconv"], p["w_out_a"],
      p["w_cconv"], p["b_cconv"], p["ln_g"], p["ln_b"], p["w_out_b"], p["b_out_b"], p["w_o"],
      jnp.zeros((n_tokens, HALF), I32), jnp.zeros((n_tokens, HALF), I32))


def _sample_mixer_kernel(x_ref, ada_ref, st_a_ref, st_b_ref, gpre1_ref, gpost1_ref, gpre2_ref, w_in_ref,
                         w_sconv_ref, w_out_a_ref, w_cconv_ref, b_cconv_ref, ln_g_ref, ln_b_ref,
                         w_out_b_ref, b_out_b_ref, w_o_ref, ha_all_ref, hb_all_ref,
                         x1_ref, ha_ref, hb_ref, new_a_ref, new_b_ref):
    del ha_all_ref, hb_all_ref
    d = D_MODEL
    x = x_ref[...]
    h = (_rmsnorm(x, gpre1_ref[...]) * (1.0 + ada_ref[1, 0]) + ada_ref[0, 0]).astype(BF16)

    def proj(i):
        return jnp.dot(h, w_in_ref[:, i * d:(i + 1) * d], preferred_element_type=F32)

    u = proj(2) * proj(0)
    v = w_sconv_ref[K_SCONV - 1:K_SCONV, :] * u
    for k in range(K_SCONV - 1):
        v = v + w_sconv_ref[k:k + 1, :] * st_a_ref[:, k, :]
    new_a_ref[:, :K_SCONV - 2, :] = st_a_ref[:, 1:, :]
    new_a_ref[:, K_SCONV - 2, :] = u
    y_a = _mm(proj(1) * v, w_out_a_ref[...])

    glu = proj(3) * _sigmoid(proj(4))
    cv = w_cconv_ref[K_CCONV - 1:K_CCONV, :] * glu + b_cconv_ref[...]
    for k in range(K_CCONV - 1):
        cv = cv + w_cconv_ref[k:k + 1, :] * st_b_ref[:, k, :]
    new_b_ref[:, :K_CCONV - 2, :] = st_b_ref[:, 1:, :]
    new_b_ref[:, K_CCONV - 2, :] = glu
    y_b = _mm(_silu(_layernorm(cv, ln_g_ref[...], ln_b_ref[...])), w_out_b_ref[...]) + b_out_b_ref[...]

    _mixer_tail(x, ada_ref, y_a, y_b, _sigmoid(proj(5)), _sigmoid(proj(6)), w_o_ref, gpost1_ref,
                gpre2_ref, x1_ref, ha_ref, hb_ref)


def _sample_mixer(x, ada4, st_a, st_b, ha_all, hb_all, first_token, p):
    n, d = x.shape
    tb = 32
    first_block = first_token // tb
    row = lambda: _resident((1, d))
    tok = lambda w: pl.BlockSpec((tb, w), lambda i: (i, 0))
    state = lambda k: pl.BlockSpec((tb, k - 1, d), lambda i: (i, 0, 0))
    packed = lambda: pl.BlockSpec((tb, HALF), lambda i: (first_block + i, 0))
    return pl.pallas_call(
        _sample_mixer_kernel,
        grid=(n // tb,),
        in_specs=[
            tok(d),
            pl.BlockSpec((6, 1, tb, d), lambda i: (0, 0, i, 0)),
            state(K_SCONV), state(K_CCONV),
            row(), row(), row(),
            _resident(p["w_in"].shape),
            _resident((K_SCONV, d)),
            _resident((d, d)),
            _resident((K_CCONV, d)),
            row(), row(), row(),
            _resident((d, d)),
            row(),
            _resident((d, d)),
            pl.BlockSpec(memory_space=pl.ANY), pl.BlockSpec(memory_space=pl.ANY),
        ],
        out_specs=[tok(d), packed(), packed(), state(K_SCONV), state(K_CCONV)],
        out_shape=[
            jax.ShapeDtypeStruct((n, d), F32),
            jax.ShapeDtypeStruct(ha_all.shape, I32),
            jax.ShapeDtypeStruct(hb_all.shape, I32),
            jax.ShapeDtypeStruct(st_a.shape, F32),
            jax.ShapeDtypeStruct(st_b.shape, F32),
        ],
        input_output_aliases={17: 1, 18: 2},
        compiler_params=_params("arbitrary"),
        name="sample_mixer",
    )(x, ada4, st_a, st_b, p["g_pre1"], p["g_post1"], p["g_pre2"], p["w_in"], p["w_sconv"], p["w_out_a"],
      p["w_cconv"], p["b_cconv"], p["ln_g"], p["ln_b"], p["w_out_b"], p["b_out_b"], p["w_o"], ha_all, hb_all)


def _first_index_of_max(work, iota, axis, limit):
    m = jnp.max(work, axis=axis, keepdims=True)
    return m, jnp.min(jnp.where(work == m, iota, limit), axis=axis, keepdims=True)


def _router_kernel(ha_ref, hb_ref, wr_ref, bias_ref, eid_ref, w8_ref, cnt_ref):
    tm = ha_ref.shape[0]
    hi, lo = _unpack_bf16_pairs(jnp.concatenate([ha_ref[...], hb_ref[...]], axis=1))
    nt = (((1,), (1,)), ((), ()))
    logits = (lax.dot_general(wr_ref[:, :PACKED], hi.astype(BF16), nt, preferred_element_type=F32)
              + lax.dot_general(wr_ref[:, PACKED:], lo.astype(BF16), nt, preferred_element_type=F32))
    scores = _sigmoid(logits)
    sel = scores + bias_ref[...]

    sel3 = sel.reshape(N_GROUPS, GROUP_SIZE, tm)
    io3 = lax.broadcasted_iota(I32, sel3.shape, 1)
    m1, i1 = _first_index_of_max(sel3, io3, 1, GROUP_SIZE)
    m2 = jnp.max(jnp.where(io3 == i1, NEG_INF, sel3), axis=1, keepdims=True)
    gscore = (m1 + m2).reshape(N_GROUPS, tm)

    iog = lax.broadcasted_iota(I32, gscore.shape, 0)
    gmask = jnp.zeros(gscore.shape, jnp.bool_)
    for _ in range(TOPK_GROUPS):
        _, gi = _first_index_of_max(gscore, iog, 0, N_GROUPS)
        pick = iog == gi
        gmask = jnp.logical_or(gmask, pick)
        gscore = jnp.where(pick, NEG_INF, gscore)

    work = jnp.where(gmask.reshape(N_GROUPS, 1, tm), sel3, NEG_INF).reshape(N_EXPERTS, tm)
    ioe = lax.broadcasted_iota(I32, work.shape, 0)
    chosen = jnp.zeros(work.shape, jnp.bool_)
    eids, ws = [], []
    for _ in range(TOP_K):
        _, ei = _first_index_of_max(work, ioe, 0, N_EXPERTS)
        pick = ioe == ei
        eids.append(ei)
        ws.append(jnp.sum(jnp.where(pick, scores, 0.0), axis=0, keepdims=True))
        chosen = jnp.logical_or(chosen, pick)
        work = jnp.where(pick, NEG_INF, work)

    w = jnp.concatenate(ws, axis=0)
    eid_ref[...] = jnp.concatenate(eids, axis=0)
    w8_ref[...] = w / jnp.sum(w, axis=0, keepdims=True) * ROUTED_SCALE

    @pl.when(pl.program_id(0) == 0)
    def _():
        cnt_ref[...] = jnp.zeros(cnt_ref.shape, F32)

    cnt_ref[...] += jnp.sum(chosen.astype(F32), axis=1, keepdims=True)


def _router(ha, hb, w_router_t, e_bias_col, tm):
    t = ha.shape[0]
    d = D_MODEL
    picks = lambda: pl.BlockSpec((TOP_K, tm), lambda i: (0, i))
    half_rows = lambda: pl.BlockSpec((tm, HALF), lambda i: (i, 0))
    return pl.pallas_call(
        _router_kernel,
        grid=(t // tm,),
        in_specs=[
            half_rows(), half_rows(),
            _resident((N_EXPERTS, d)),
            _resident((N_EXPERTS, 1)),
        ],
        out_specs=[picks(), picks(), pl.BlockSpec((N_EXPERTS, LANES), lambda i: (0, 0))],
        out_shape=[
            jax.ShapeDtypeStruct((TOP_K, t), I32),
            jax.ShapeDtypeStruct((TOP_K, t), F32),
            jax.ShapeDtypeStruct((N_EXPERTS, LANES), F32),
        ],
        compiler_params=_params("arbitrary"),
        name="router",
    )(ha, hb, w_router_t, e_bias_col)


def _position_kernel(eid_ref, off_ref, dest_ref, run_ref):
    tm = eid_ref.shape[1]

    @pl.when(pl.program_id(0) == 0)
    def _():
        run_ref[...] = off_ref[...]

    eid = eid_ref[...]
    ioe = lax.broadcasted_iota(I32, (N_EXPERTS, tm), 0)
    picks = [ioe == eid[k:k + 1, :] for k in range(TOP_K)]
    chosen = functools.reduce(jnp.logical_or, picks)
    before = (lax.broadcasted_iota(I32, (tm, tm), 0) < lax.broadcasted_iota(I32, (tm, tm), 1))
    rank = jnp.dot(chosen.astype(BF16), before.astype(BF16), preferred_element_type=F32)
    pos = run_ref[...] + rank
    dest = [jnp.sum(jnp.where(pk, pos, 0.0), axis=0, keepdims=True) for pk in picks]
    dest_ref[...] = jnp.concatenate(dest, axis=0).astype(I32)
    run_ref[...] += jnp.sum(chosen.astype(F32), axis=1, keepdims=True)


def _positions(eid, off_col):
    t = eid.shape[1]
    tm = POSITION_TILE
    return pl.pallas_call(
        _position_kernel,
        grid=(t // tm,),
        in_specs=[pl.BlockSpec((TOP_K, tm), lambda i: (0, i)), _resident((N_EXPERTS, 1))],
        out_specs=pl.BlockSpec((TOP_K, tm), lambda i: (0, i)),
        out_shape=jax.ShapeDtypeStruct((TOP_K, t), I32),
        scratch_shapes=[pltpu.VMEM((N_EXPERTS, 1), F32)],
        compiler_params=_params("arbitrary"),
        name="positions",
    )(eid, off_col)


def _sc_mesh():
    return plsc.VectorSubcoreMesh(core_axis_name="core", subcore_axis_name="subcore")


def _dispatch_rows(xa, xb, dest, n_rows):
    t, width = xa.shape
    n_picks = dest.shape[0]
    out = jax.ShapeDtypeStruct((n_rows, width), xa.dtype)

    @functools.partial(pl.kernel, out_type=[out, out], mesh=_sc_mesh(), scratch_types=[])
    def scatter(xa_hbm, xb_hbm, i_hbm, oa_hbm, ob_hbm):
        for x_hbm, o_hbm in ((xa_hbm, oa_hbm), (xb_hbm, ob_hbm)):
            def body(x_vmem, i_vmem, o_hbm=o_hbm):
                for k in range(n_picks):
                    pltpu.sync_copy(x_vmem, o_hbm.at[i_vmem.at[k]])

            pltpu.emit_pipeline(
                body,
                grid=(t // SC_WINDOW,),
                in_specs=[pl.BlockSpec((SC_WINDOW, width), lambda i: (i, 0)),
                          pl.BlockSpec((n_picks, SC_WINDOW), lambda i: (0, i))],
                out_specs=[],
                core_axis_name=("core", "subcore"),
                dimension_semantics=(pltpu.PARALLEL,),
            )(x_hbm, i_hbm)

    return scatter(xa, xb, dest)


def _gather_rows(ya, yb, idx):
    n = idx.shape[1]
    width = ya.shape[1]
    out = jax.ShapeDtypeStruct((n, width), ya.dtype)

    @functools.partial(pl.kernel, out_type=[out, out], mesh=_sc_mesh(), scratch_types=[])
    def gather(ya_hbm, yb_hbm, i_hbm, oa_hbm, ob_hbm):
        for y_hbm, o_hbm in ((ya_hbm, oa_hbm), (yb_hbm, ob_hbm)):
            def body(i_vmem, o_vmem, y_hbm=y_hbm):
                pltpu.sync_copy(y_hbm.at[i_vmem.at[0]], o_vmem)

            pltpu.emit_pipeline(
                body,
                grid=(n // SC_WINDOW,),
                in_specs=[pl.BlockSpec((1, SC_WINDOW), lambda i: (0, i))],
                out_specs=[pl.BlockSpec((SC_WINDOW, width), lambda i: (i, 0))],
                core_axis_name=("core", "subcore"),
                dimension_semantics=(pltpu.PARALLEL,),
            )(i_hbm, o_hbm)

    return gather(ya, yb, idx)


def _expert_kernel(first_tile_ref, n_tiles_ref, total_ref, tile_rows_ref, xa_hbm, xb_hbm, w1_ref, w3_ref, w2_ref,
                   ya_hbm, yb_hbm, xa_buf, xb_buf, ya_buf, yb_buf, in_sem, out_sem, w1_bf, w3_bf, w2_bf):
    e = pl.program_id(0)
    tm = EXPERT_TILE
    ring = EXPERT_RING
    total = total_ref[0]
    first_tile = first_tile_ref[e]

    def pieces(g, slot, bufs_hbm, sems, to_vmem):
        out = []
        for r0 in range(0, tm, EXPERT_COPY_ROWS):
            hbm_rows = pl.ds(pl.multiple_of(g * tm + r0, EXPERT_COPY_ROWS), EXPERT_COPY_ROWS)
            for i, (buf, hbm) in enumerate(bufs_hbm):
                vmem = buf.at[slot, pl.ds(r0, EXPERT_COPY_ROWS)]
                src, dst = (hbm.at[hbm_rows], vmem) if to_vmem else (vmem, hbm.at[hbm_rows])
                out.append((r0, pltpu.make_async_copy(src, dst, sems.at[i, slot])))
        return out

    def for_live_pieces(g, bufs_hbm, sems, to_vmem, act):
        slot = g % ring
        live_rows = tile_rows_ref[g]
        for r0, copy in pieces(g, slot, bufs_hbm, sems, to_vmem):
            pl.when(r0 < live_rows)(functools.partial(act, copy))

    ins = ((xa_buf, xa_hbm), (xb_buf, xb_hbm))
    outs = ((ya_buf, ya_hbm), (yb_buf, yb_hbm))
    start_in = lambda g: for_live_pieces(g, ins, in_sem, True, lambda c: c.start())
    wait_in = lambda g: for_live_pieces(g, ins, in_sem, True, lambda c: c.wait())
    start_out = lambda g: for_live_pieces(g, outs, out_sem, False, lambda c: c.start())
    wait_out = lambda g: for_live_pieces(g, outs, out_sem, False, lambda c: c.wait())

    @pl.when(n_tiles_ref[e] > 0)
    def _():
        w1_bf[...] = w1_ref[0].astype(BF16)
        w3_bf[...] = w3_ref[0].astype(BF16)
        w2_bf[...] = w2_ref[0].astype(BF16)

    def tile(t, carry):
        g = first_tile + t
        slot = g % ring

        @pl.when(g == 0)
        def _():
            for ahead in range(ring - 1):
                pl.when(ahead < total)(functools.partial(start_in, ahead))

        pl.when(g + ring - 1 < total)(functools.partial(start_in, g + ring - 1))
        wait_in(g)
        pl.when(g >= ring)(functools.partial(wait_out, g - ring))

        hi, lo = _unpack_bf16_pairs(jnp.concatenate([xa_buf[slot], xb_buf[slot]], axis=1))
        hi = hi.astype(BF16)
        lo = lo.astype(BF16)

        def up(w_bf):
            return (jnp.dot(hi, w_bf[:PACKED, :], preferred_element_type=F32)
                    + jnp.dot(lo, w_bf[PACKED:, :], preferred_element_type=F32))

        act = _silu(up(w1_bf)) * up(w3_bf)
        words = _pack_bf16_pairs(_mm(act, w2_bf[...]))
        ya_buf[slot] = words[:, :HALF]
        yb_buf[slot] = words[:, HALF:]
        start_out(g)
        return carry

    lax.fori_loop(0, n_tiles_ref[e], tile, 0)

    @pl.when(e == pl.num_programs(0) - 1)
    def _():
        for back in range(ring, 0, -1):
            pl.when(total >= back)(functools.partial(wait_out, total - back))


def _experts(xa, xb, first_tile, n_tiles, total_tiles, tile_rows, w1, w3, w2):
    rows = xa.shape[0]
    d, f = D_MODEL, D_EXPERT
    tm = EXPERT_TILE
    weights = lambda a, b: pl.BlockSpec((1, a, b), lambda e, *_: (e, 0, 0))
    ring = lambda: pltpu.VMEM((EXPERT_RING, tm, HALF), I32)
    sems = lambda: pltpu.SemaphoreType.DMA((2, EXPERT_RING))
    return pl.pallas_call(
        _expert_kernel,
        grid_spec=pltpu.PrefetchScalarGridSpec(
            num_scalar_prefetch=4,
            grid=(N_EXPERTS,),
            in_specs=[
                pl.BlockSpec(memory_space=pl.ANY), pl.BlockSpec(memory_space=pl.ANY),
                weights(d, f), weights(d, f), weights(f, d),
            ],
            out_specs=[pl.BlockSpec(memory_space=pl.ANY), pl.BlockSpec(memory_space=pl.ANY)],
            scratch_shapes=[
                ring(), ring(), ring(), ring(), sems(), sems(),
                pltpu.VMEM((d, f), BF16), pltpu.VMEM((d, f), BF16), pltpu.VMEM((f, d), BF16),
            ],
        ),
        out_shape=[jax.ShapeDtypeStruct((rows, HALF), I32), jax.ShapeDtypeStruct((rows, HALF), I32)],
        compiler_params=_params("arbitrary"),
        name="experts",
    )(first_tile, n_tiles, total_tiles, tile_rows, xa, xb, w1, w3, w2)


def _final_kernel(ha_ref, hb_ref, x1_ref, ada_ref, gpost2_ref, w8_ref, yga_ref, ygb_ref, ws1_ref, ws3_ref,
                  ws2_ref, out_ref):
    hi, lo = _unpack_bf16_pairs(jnp.concatenate([ha_ref[...], hb_ref[...]], axis=1))
    hi = hi.astype(BF16)
    lo = lo.astype(BF16)

    def up(w_ref):
        return (jnp.dot(hi, w_ref[:PACKED, :], preferred_element_type=F32)
                + jnp.dot(lo, w_ref[PACKED:, :], preferred_element_type=F32))

    f = _mm(_silu(up(ws1_ref)) * up(ws3_ref), ws2_ref[...])
    for k in range(TOP_K):
        hi, lo = _unpack_bf16_pairs(jnp.concatenate([yga_ref[k], ygb_ref[k]], axis=1))
        f = f + w8_ref[:, k:k + 1] * jnp.concatenate([hi, lo], axis=1)
    out_ref[...] = x1_ref[...] + ada_ref[5, 0] * _rmsnorm(f, gpost2_ref[...])


def _final(ha, hb, x1, ada4, tokens_per_ada_block, w8_t, yga, ygb, first_token, p, tm):
    t, d = x1.shape
    f = D_EXPERT
    first_block = first_token // tm
    tok = lambda: pl.BlockSpec((tm, d), lambda i: (i, 0))
    if ada4.shape[2] == 1:
        ada_spec = pl.BlockSpec((6, 1, 1, d), lambda i: (0, i * tm // tokens_per_ada_block, 0, 0))
    else:
        ada_spec = pl.BlockSpec((6, 1, tm, d), lambda i: (0, 0, i, 0))
    gathered = lambda: pl.BlockSpec((TOP_K, tm, HALF), lambda i: (0, first_block + i, 0))
    return pl.pallas_call(
        _final_kernel,
        grid=(t // tm,),
        in_specs=[
            pl.BlockSpec((tm, HALF), lambda i: (first_block + i, 0)),
            pl.BlockSpec((tm, HALF), lambda i: (first_block + i, 0)),
            tok(), ada_spec, _resident((1, d)),
            pl.BlockSpec((tm, TOP_K), lambda i: (first_block + i, 0)),
            gathered(), gathered(),
            _resident((d, f)), _resident((d, f)), _resident((f, d)),
        ],
        out_specs=tok(),
        out_shape=jax.ShapeDtypeStruct((t, d), F32),
        compiler_params=_params("arbitrary"),
        name="final",
    )(ha, hb, x1, ada4, p["g_post2"], w8_t, yga, ygb, p["ws1"], p["ws3"], p["ws2"])


def kernel(x_prompt, x_sample, state_sconv, state_cconv, c_prompt, c_sample, w_ada, b_ada, g_pre1, g_post1,
           g_pre2, g_post2, w_in, w_sconv, w_out_a, w_cconv, b_cconv, ln_g, ln_b, w_out_b, b_out_b, w_o,
           w_router, e_bias, w1, w3, w2, ws1, ws3, ws2):
    assert w_ada.shape[0] == 1, "single-layer trunk"
    nb, seq, d = x_prompt.shape
    ns = x_sample.shape[0]
    n_prompt = nb * seq
    n_tokens = n_prompt + ns
    p = {
        "g_pre1": g_pre1, "g_post1": g_post1, "g_pre2": g_pre2, "g_post2": g_post2,
        "w_in": w_in[0].astype(BF16), "w_sconv": w_sconv[0], "w_out_a": w_out_a[0].astype(BF16),
        "w_cconv": w_cconv[0], "b_cconv": b_cconv, "ln_g": ln_g, "ln_b": ln_b,
        "w_out_b": w_out_b[0].astype(BF16), "b_out_b": b_out_b, "w_o": w_o[0].astype(BF16),
        "ws1": ws1[0].astype(BF16), "ws3": ws3[0].astype(BF16), "ws2": ws2[0].astype(BF16),
    }
    w_router_t = w_router[0].T.astype(BF16)
    e_bias_col = e_bias[0].reshape(N_EXPERTS, 1)

    ada = _ada(jnp.concatenate([c_prompt, c_sample], axis=0), w_ada[0], b_ada)
    ada = ada.reshape(nb + ns, 6, d).transpose(1, 0, 2)
    ada_p = ada[:, :nb].reshape(6, nb, 1, d)
    ada_s = ada[:, nb:].reshape(6, 1, ns, d)

    x1_p, ha, hb, u_tail, glu_tail = _prompt_mixer(x_prompt, ada_p, n_tokens, p)
    x1_s, ha, hb, new_sconv_sample, new_cconv_sample = _sample_mixer(
        x_sample.reshape(ns, d), ada_s, state_sconv[0], state_cconv[0], ha, hb, n_prompt, p)

    eid, w8, cnt = _router(ha, hb, w_router_t, e_bias_col, ROUTER_TILE)
    w8_t = w8.T
    counts = cnt[:, 0].astype(I32)
    n_tiles = (counts + EXPERT_TILE - 1) // EXPERT_TILE
    tile_ends = jnp.cumsum(n_tiles)
    first_tile = tile_ends - n_tiles
    n_rows = n_tokens * TOP_K + N_EXPERTS * EXPERT_TILE
    dest = _positions(eid, (first_tile * EXPERT_TILE).astype(F32).reshape(N_EXPERTS, 1))
    tile_ids = jnp.arange(pl.cdiv(n_rows, EXPERT_TILE), dtype=I32)
    tile_owner = jnp.minimum(jnp.sum(tile_ids[:, None] >= tile_ends[None, :], axis=1), N_EXPERTS - 1)
    tile_rows = jnp.clip(counts[tile_owner] - (tile_ids - first_tile[tile_owner]) * EXPERT_TILE, 0, EXPERT_TILE)
    tile_rows = jnp.where(tile_ids < tile_ends[-1], tile_rows, 0).astype(I32)

    xa, xb = _dispatch_rows(ha, hb, dest, n_rows)
    ya, yb = _experts(xa, xb, first_tile, n_tiles, tile_ends[-1:], tile_rows, w1[0], w3[0], w2[0])
    flat = dest.reshape(1, TOP_K * n_tokens)
    yga, ygb = (g.reshape(TOP_K, n_tokens, HALF) for g in _gather_rows(ya, yb, flat))

    y_p = _final(ha, hb, x1_p.reshape(n_prompt, d), ada_p, seq, w8_t, yga, ygb, 0, p, FINAL_TILE)
    y_s = _final(ha, hb, x1_s, ada_s, 1, w8_t, yga, ygb, n_prompt, p, ns)

    new_sconv_prompt = u_tail[:, SUBLANES - (K_SCONV - 1):][None]
    new_cconv_prompt = glu_tail[:, CONV_HALO - (K_CCONV - 1):][None]
    return (y_p.reshape(nb, seq, d), y_s.reshape(ns, 1, d),
            new_sconv_prompt, new_sconv_sample[None], new_cconv_prompt, new_cconv_sample[None])
```

```python
import functools

import jax
import jax.numpy as jnp
from jax import lax
from jax.experimental import pallas as pl
from jax.experimental.pallas import tpu as pltpu
from jax.experimental.pallas import tpu_sc as plsc

D_MODEL = 1024
K_SCONV = 3
K_CCONV = 31
N_EXPERTS = 64
TOP_K = 8
N_GROUPS = 8
GROUP_SIZE = N_EXPERTS // N_GROUPS
TOPK_GROUPS = 4
D_EXPERT = 256
ROUTED_SCALE = 2.5
RMS_EPS = 1e-6
LN_EPS = 1e-5

F32 = jnp.float32
BF16 = jnp.bfloat16
I32 = jnp.int32
NEG_INF = float("-inf")

VMEM_LIMIT_BYTES_V7X = 56 * 1024 * 1024
SUBLANES = 8
LANES = 128

MIXER_TILE = 512
CONV_HALO = 32
CONV_ROW_CHUNK = 64
CONV_LANE_CHUNK = 128
PROJ_LANE_CHUNK = 256
ROUTER_TILE = 384
POSITION_TILE = 5504
EXPERT_TILE = 768
EXPERT_COPY_ROWS = 256
EXPERT_RING = 3
FINAL_TILE = 512
PACKED = D_MODEL // 2
HALF = PACKED // 2
SC_WINDOW = 128


def _params(*semantics, flags=None):
    return pltpu.CompilerParams(dimension_semantics=semantics, vmem_limit_bytes=VMEM_LIMIT_BYTES_V7X, flags=flags)


def _resident(shape):
    zeros = (0,) * len(shape)
    return pl.BlockSpec(shape, lambda *_: zeros, pipeline_mode=pl.Buffered(1))


def _sigmoid(x):
    return 0.5 * jnp.tanh(0.5 * x) + 0.5


def _silu(x):
    return x * _sigmoid(x)


def _rmsnorm(x, g):
    return x * lax.rsqrt(jnp.mean(x * x, axis=-1, keepdims=True) + RMS_EPS) * g


def _layernorm(x, g, b):
    mu = jnp.mean(x, axis=-1, keepdims=True)
    xc = x - mu
    var = jnp.mean(xc * xc, axis=-1, keepdims=True)
    return xc * lax.rsqrt(var + LN_EPS) * g + b


def _mm(a, w):
    return jnp.dot(a.astype(BF16), w.astype(BF16), preferred_element_type=F32)


def _pack_bf16_pairs(v):
    n = v.shape[1] // 2
    bits = lax.bitcast_convert_type(v.astype(BF16).astype(F32), I32)
    return bits[:, :n] | lax.shift_right_logical(bits[:, n:], 16)


def _unpack_bf16_pairs(words):
    hi = lax.bitcast_convert_type(words & jnp.int32(-65536), F32)
    lo = lax.bitcast_convert_type(lax.shift_left(words, 16), F32)
    return hi, lo


def _ada_kernel(c_ref, w_ref, b_ref, o_ref):
    o_ref[...] = _mm(_silu(c_ref[...]), w_ref[...]) + b_ref[...]


def _ada(c, w_ada, b_ada):
    n = c.shape[0]
    width = w_ada.shape[1]
    bn = D_MODEL
    return pl.pallas_call(
        _ada_kernel,
        grid=(width // bn,),
        in_specs=[
            pl.BlockSpec((n, D_MODEL), lambda i: (0, 0)),
            pl.BlockSpec((D_MODEL, bn), lambda i: (0, i)),
            pl.BlockSpec((1, bn), lambda i: (0, i)),
        ],
        out_specs=pl.BlockSpec((n, bn), lambda i: (0, i)),
        out_shape=jax.ShapeDtypeStruct((n, width), F32),
        compiler_params=_params("arbitrary"),
        name="ada",
    )(c, w_ada, b_ada)


def _mixer_tail(x, ada_ref, y_a, y_b, gate_a, gate_b, w_o_ref, gpost1_ref, gpre2_ref, x1_ref, ha_ref, hb_ref):
    m = gate_a * y_a + gate_b * y_b
    t = _mm(m, w_o_ref[...])
    x1 = x + ada_ref[2, 0] * _rmsnorm(t, gpost1_ref[...])
    h2 = _rmsnorm(x1, gpre2_ref[...]) * (1.0 + ada_ref[4, 0]) + ada_ref[3, 0]
    x1_ref[...] = x1.reshape(x1_ref.shape)
    words = _pack_bf16_pairs(h2)
    ha_ref[...] = words[:, :HALF].reshape(ha_ref.shape)
    hb_ref[...] = words[:, HALF:].reshape(hb_ref.shape)


def _prompt_mixer_kernel(x_ref, ada_ref, gpre1_ref, gpost1_ref, gpre2_ref, w_in_ref, w_sconv_ref,
                         w_out_a_ref, w_cconv_ref, b_cconv_ref, ln_g_ref, ln_b_ref, w_out_b_ref,
                         b_out_b_ref, w_o_ref, ha_init_ref, hb_init_ref,
                         x1_ref, ha_ref, hb_ref, u_tail_ref, glu_tail_ref,
                         u_ext, glu_ext, cv_buf, h_buf, av_buf, ga_buf, gb_buf):
    del ha_init_ref, hb_init_ref
    tm = x_ref.shape[1]
    d = D_MODEL

    @pl.when(pl.program_id(1) == 0)
    def _():
        u_ext[pl.ds(0, SUBLANES), :] = jnp.zeros((SUBLANES, d), F32)
        glu_ext[pl.ds(0, CONV_HALO), :] = jnp.zeros((CONV_HALO, d), F32)

    h_buf[...] = (_rmsnorm(x_ref[0], gpre1_ref[...]) * (1.0 + ada_ref[1, 0]) + ada_ref[0, 0]).astype(BF16)

    def proj(i, lanes=slice(0, d)):
        return jnp.dot(h_buf[...], w_in_ref[:, i * d + lanes.start:i * d + lanes.stop],
                       preferred_element_type=F32)

    glu = proj(3) * _sigmoid(proj(4))
    glu_ext[pl.ds(CONV_HALO, tm), :] = glu
    glu_tail_ref[0] = glu[tm - CONV_HALO:, :]
    first = CONV_HALO - (K_CCONV - 1)

    def conv_piece(r0, lanes):
        acc = None
        for res in range(SUBLANES):
            rows = CONV_ROW_CHUNK + (SUBLANES if res else 0)
            part = None
            for k in range(K_CCONV):
                if (first + k) % SUBLANES != res:
                    continue
                start = r0 + first + k - res
                term = w_cconv_ref[k:k + 1, lanes] * glu_ext[start:start + rows, lanes]
                part = term if part is None else part + term
            part = part[res:res + CONV_ROW_CHUNK, :]
            acc = part if acc is None else acc + part
        cv_buf[r0:r0 + CONV_ROW_CHUNK, lanes] = acc

    def branch_a(lanes):
        u = proj(2, lanes) * proj(0, lanes)
        u_ext[pl.ds(SUBLANES, tm), lanes] = u
        v = (w_sconv_ref[0:1, lanes] * u_ext[pl.ds(SUBLANES - 2, tm), lanes]
             + w_sconv_ref[1:2, lanes] * u_ext[pl.ds(SUBLANES - 1, tm), lanes]
             + w_sconv_ref[2:3, lanes] * u)
        u_tail_ref[0, :, lanes] = u[tm - SUBLANES:, :]
        u_ext[pl.ds(0, SUBLANES), lanes] = u[tm - SUBLANES:, :]
        av_buf[:, lanes] = (proj(1, lanes) * v).astype(BF16)

    def gates(lanes):
        ga_buf[:, lanes] = _sigmoid(proj(5, lanes))
        gb_buf[:, lanes] = _sigmoid(proj(6, lanes))

    for c in range(0, d, PROJ_LANE_CHUNK):
        lanes = slice(c, c + PROJ_LANE_CHUNK)

        @pl.when(pl.program_id(1) + c >= 0)
        def _(lanes=lanes):
            for r0 in range(0, tm, CONV_ROW_CHUNK):
                for l0 in range(lanes.start, lanes.stop, CONV_LANE_CHUNK):
                    conv_piece(r0, slice(l0, l0 + CONV_LANE_CHUNK))
            branch_a(lanes)
            gates(lanes)

    glu_ext[pl.ds(0, CONV_HALO), :] = glu_ext[pl.ds(tm, CONV_HALO), :]

    y_a = jnp.dot(av_buf[...], w_out_a_ref[...], preferred_element_type=F32)
    cv = cv_buf[...] + b_cconv_ref[...]
    y_b = _mm(_silu(_layernorm(cv, ln_g_ref[...], ln_b_ref[...])), w_out_b_ref[...]) + b_out_b_ref[...]

    _mixer_tail(x_ref[0], ada_ref, y_a, y_b, ga_buf[...], gb_buf[...], w_o_ref, gpost1_ref, gpre2_ref,
                x1_ref, ha_ref, hb_ref)


def _prompt_mixer(x, ada4, n_tokens, p):
    nb, seq, d = x.shape
    tm = MIXER_TILE
    row = lambda: _resident((1, d))
    tok = lambda w: pl.BlockSpec((1, tm, w), lambda b, j: (b, j, 0))
    packed = lambda: pl.BlockSpec((tm, HALF), lambda b, j: (b * (seq // tm) + j, 0))
    return pl.pallas_call(
        _prompt_mixer_kernel,
        grid=(nb, seq // tm),
        in_specs=[
            tok(d),
            pl.BlockSpec((6, 1, 1, d), lambda b, j: (0, b, 0, 0)),
            row(), row(), row(),
            _resident(p["w_in"].shape),
            _resident((K_SCONV, d)),
            _resident((d, d)),
            _resident((K_CCONV, d)),
            row(), row(), row(),
            _resident((d, d)),
            row(),
            _resident((d, d)),
            pl.BlockSpec(memory_space=pl.ANY), pl.BlockSpec(memory_space=pl.ANY),
        ],
        out_specs=[
            tok(d), packed(), packed(),
            pl.BlockSpec((1, SUBLANES, d), lambda b, j: (b, 0, 0)),
            pl.BlockSpec((1, CONV_HALO, d), lambda b, j: (b, 0, 0)),
        ],
        out_shape=[
            jax.ShapeDtypeStruct((nb, seq, d), F32),
            jax.ShapeDtypeStruct((n_tokens, HALF), I32),
            jax.ShapeDtypeStruct((n_tokens, HALF), I32),
            jax.ShapeDtypeStruct((nb, SUBLANES, d), F32),
            jax.ShapeDtypeStruct((nb, CONV_HALO, d), F32),
        ],
        scratch_shapes=[
            pltpu.VMEM((tm + SUBLANES, d), F32),
            pltpu.VMEM((tm + CONV_HALO + SUBLANES, d), F32),
            pltpu.VMEM((tm, d), F32),
            pltpu.VMEM((tm, d), BF16),
            pltpu.VMEM((tm, d), BF16),
            pltpu.VMEM((tm, d), F32),
            pltpu.VMEM((tm, d), F32),
        ],
        input_output_aliases={15: 1, 16: 2},
        compiler_params=_params("arbitrary", "arbitrary"),
        name="prompt_mixer",
    )(x, ada4, p["g_pre1"], p["g_post1"], p["g_pre2"], p["w_in"], p["w_sconv"], p["w_out_a"],
      p["w_cconv"], p["b_cconv"], p["ln_g"], p["ln_b"], p["w_out_b"], p["b_out_b"], p["w_o"],
      jnp.zeros((n_tokens, HALF), I32), jnp.zeros((n_tokens, HALF), I32))


def _sample_mixer_kernel(x_ref, ada_ref, st_a_ref, st_b_ref, gpre1_ref, gpost1_ref, gpre2_ref, w_in_ref,
                         w_sconv_ref, w_out_a_ref, w_cconv_ref, b_cconv_ref, ln_g_ref, ln_b_ref,
                         w_out_b_ref, b_out_b_ref, w_o_ref, ha_all_ref, hb_all_ref,
                         x1_ref, ha_ref, hb_ref, new_a_ref, new_b_ref):
    del ha_all_ref, hb_all_ref
    d = D_MODEL
    x = x_ref[...]
    h = (_rmsnorm(x, gpre1_ref[...]) * (1.0 + ada_ref[1, 0]) + ada_ref[0, 0]).astype(BF16)

    def proj(i):
        return jnp.dot(h, w_in_ref[:, i * d:(i + 1) * d], preferred_element_type=F32)

    u = proj(2) * proj(0)
    v = w_sconv_ref[K_SCONV - 1:K_SCONV, :] * u
    for k in range(K_SCONV - 1):
        v = v + w_sconv_ref[k:k + 1, :] * st_a_ref[:, k, :]
    new_a_ref[:, :K_SCONV - 2, :] = st_a_ref[:, 1:, :]
    new_a_ref[:, K_SCONV - 2, :] = u
    y_a = _mm(proj(1) * v, w_out_a_ref[...])

    glu = proj(3) * _sigmoid(proj(4))
    cv = w_cconv_ref[K_CCONV - 1:K_CCONV, :] * glu + b_cconv_ref[...]
    for k in range(K_CCONV - 1):
        cv = cv + w_cconv_ref[k:k + 1, :] * st_b_ref[:, k, :]
    new_b_ref[:, :K_CCONV - 2, :] = st_b_ref[:, 1:, :]
    new_b_ref[:, K_CCONV - 2, :] = glu
    y_b = _mm(_silu(_layernorm(cv, ln_g_ref[...], ln_b_ref[...])), w_out_b_ref[...]) + b_out_b_ref[...]

    _mixer_tail(x, ada_ref, y_a, y_b, _sigmoid(proj(5)), _sigmoid(proj(6)), w_o_ref, gpost1_ref,
                gpre2_ref, x1_ref, ha_ref, hb_ref)


def _sample_mixer(x, ada4, st_a, st_b, ha_all, hb_all, first_token, p):
    n, d = x.shape
    tb = 32
    first_block = first_token // tb
    row = lambda: _resident((1, d))
    tok = lambda w: pl.BlockSpec((tb, w), lambda i: (i, 0))
    state = lambda k: pl.BlockSpec((tb, k - 1, d), lambda i: (i, 0, 0))
    packed = lambda: pl.BlockSpec((tb, HALF), lambda i: (first_block + i, 0))
    return pl.pallas_call(
        _sample_mixer_kernel,
        grid=(n // tb,),
        in_specs=[
            tok(d),
            pl.BlockSpec((6, 1, tb, d), lambda i: (0, 0, i, 0)),
            state(K_SCONV), state(K_CCONV),
            row(), row(), row(),
            _resident(p["w_in"].shape),
            _resident((K_SCONV, d)),
            _resident((d, d)),
            _resident((K_CCONV, d)),
            row(), row(), row(),
            _resident((d, d)),
            row(),
            _resident((d, d)),
            pl.BlockSpec(memory_space=pl.ANY), pl.BlockSpec(memory_space=pl.ANY),
        ],
        out_specs=[tok(d), packed(), packed(), state(K_SCONV), state(K_CCONV)],
        out_shape=[
            jax.ShapeDtypeStruct((n, d), F32),
            jax.ShapeDtypeStruct(ha_all.shape, I32),
            jax.ShapeDtypeStruct(hb_all.shape, I32),
            jax.ShapeDtypeStruct(st_a.shape, F32),
            jax.ShapeDtypeStruct(st_b.shape, F32),
        ],
        input_output_aliases={17: 1, 18: 2},
        compiler_params=_params("arbitrary"),
        name="sample_mixer",
    )(x, ada4, st_a, st_b, p["g_pre1"], p["g_post1"], p["g_pre2"], p["w_in"], p["w_sconv"], p["w_out_a"],
      p["w_cconv"], p["b_cconv"], p["ln_g"], p["ln_b"], p["w_out_b"], p["b_out_b"], p["w_o"], ha_all, hb_all)


def _first_index_of_max(work, iota, axis, limit):
    m = jnp.max(work, axis=axis, keepdims=True)
    return m, jnp.min(jnp.where(work == m, iota, limit), axis=axis, keepdims=True)


def _router_kernel(ha_ref, hb_ref, wr_ref, bias_ref, eid_ref, rank_ref, w8_ref, cnt_ref):
    tm = ha_ref.shape[0]
    hi, lo = _unpack_bf16_pairs(jnp.concatenate([ha_ref[...], hb_ref[...]], axis=1))
    nt = (((1,), (1,)), ((), ()))
    logits = (lax.dot_general(wr_ref[:, :PACKED], hi.astype(BF16), nt, preferred_element_type=F32)
              + lax.dot_general(wr_ref[:, PACKED:], lo.astype(BF16), nt, preferred_element_type=F32))
    scores = _sigmoid(logits)
    sel = scores + bias_ref[...]

    sel3 = sel.reshape(N_GROUPS, GROUP_SIZE, tm)
    io3 = lax.broadcasted_iota(I32, sel3.shape, 1)
    m1, i1 = _first_index_of_max(sel3, io3, 1, GROUP_SIZE)
    m2 = jnp.max(jnp.where(io3 == i1, NEG_INF, sel3), axis=1, keepdims=True)
    gscore = (m1 + m2).reshape(N_GROUPS, tm)

    iog = lax.broadcasted_iota(I32, gscore.shape, 0)
    gmask = jnp.zeros(gscore.shape, jnp.bool_)
    for _ in range(TOPK_GROUPS):
        _, gi = _first_index_of_max(gscore, iog, 0, N_GROUPS)
        pick = iog == gi
        gmask = jnp.logical_or(gmask, pick)
        gscore = jnp.where(pick, NEG_INF, gscore)

    work = jnp.where(gmask.reshape(N_GROUPS, 1, tm), sel3, NEG_INF).reshape(N_EXPERTS, tm)
    ioe = lax.broadcasted_iota(I32, work.shape, 0)
    chosen = jnp.zeros(work.shape, jnp.bool_)
    eids, ws, picks = [], [], []
    for _ in range(TOP_K):
        _, ei = _first_index_of_max(work, ioe, 0, N_EXPERTS)
        pick = ioe == ei
        eids.append(ei)
        picks.append(pick)
        ws.append(jnp.sum(jnp.where(pick, scores, 0.0), axis=0, keepdims=True))
        chosen = jnp.logical_or(chosen, pick)
        work = jnp.where(pick, NEG_INF, work)

    w = jnp.concatenate(ws, axis=0)
    eid_ref[...] = jnp.concatenate(eids, axis=0)
    w8_ref[...] = (w / jnp.sum(w, axis=0, keepdims=True) * ROUTED_SCALE).T

    @pl.when(pl.program_id(0) == 0)
    def _():
        cnt_ref[...] = jnp.zeros(cnt_ref.shape, F32)

    before = lax.broadcasted_iota(I32, (tm, tm), 0) < lax.broadcasted_iota(I32, (tm, tm), 1)
    seen = cnt_ref[:, 0:1] + jnp.dot(chosen.astype(BF16), before.astype(BF16), preferred_element_type=F32)
    rank_ref[...] = jnp.concatenate(
        [jnp.sum(jnp.where(pick, seen, 0.0), axis=0, keepdims=True) for pick in picks], axis=0)
    cnt_ref[...] += jnp.sum(chosen.astype(F32), axis=1, keepdims=True)


def _router(ha, hb, w_router_t, e_bias_col, tm):
    t = ha.shape[0]
    d = D_MODEL
    picks = lambda: pl.BlockSpec((TOP_K, tm), lambda i: (0, i))
    half_rows = lambda: pl.BlockSpec((tm, HALF), lambda i: (i, 0))
    return pl.pallas_call(
        _router_kernel,
        grid=(t // tm,),
        in_specs=[
            half_rows(), half_rows(),
            _resident((N_EXPERTS, d)),
            _resident((N_EXPERTS, 1)),
        ],
        out_specs=[picks(), picks(), pl.BlockSpec((tm, TOP_K), lambda i: (i, 0)),
                   pl.BlockSpec((N_EXPERTS, LANES), lambda i: (0, 0))],
        out_shape=[
            jax.ShapeDtypeStruct((TOP_K, t), I32),
            jax.ShapeDtypeStruct((TOP_K, t), F32),
            jax.ShapeDtypeStruct((t, TOP_K), F32),
            jax.ShapeDtypeStruct((N_EXPERTS, LANES), F32),
        ],
        compiler_params=_params("arbitrary"),
        name="router",
    )(ha, hb, w_router_t, e_bias_col)


def _offset_kernel(off_ref, eid_ref, rank_ref, dest_ref):
    eid = eid_ref[...]
    dest = rank_ref[...].astype(I32)
    for e in range(N_EXPERTS):
        dest = dest + jnp.where(eid == e, off_ref[e], 0)
    dest_ref[...] = dest


def _positions(eid, rank, first_row):
    t = eid.shape[1]
    tm = POSITION_TILE
    picks = lambda: pl.BlockSpec((TOP_K, tm), lambda i, off: (0, i))
    return pl.pallas_call(
        _offset_kernel,
        grid_spec=pltpu.PrefetchScalarGridSpec(
            num_scalar_prefetch=1, grid=(t // tm,), in_specs=[picks(), picks()], out_specs=picks()),
        out_shape=jax.ShapeDtypeStruct((TOP_K, t), I32),
        compiler_params=_params("arbitrary"),
        name="positions",
    )(first_row, eid, rank)


def _sc_mesh():
    return plsc.VectorSubcoreMesh(core_axis_name="core", subcore_axis_name="subcore")


def _dispatch_rows(xa, xb, dest, n_rows):
    t, width = xa.shape
    n_picks = dest.shape[0]
    out = jax.ShapeDtypeStruct((n_rows, width), xa.dtype)

    @functools.partial(pl.kernel, out_type=[out, out], mesh=_sc_mesh(), scratch_types=[])
    def scatter(xa_hbm, xb_hbm, i_hbm, oa_hbm, ob_hbm):
        for x_hbm, o_hbm in ((xa_hbm, oa_hbm), (xb_hbm, ob_hbm)):
            def body(x_vmem, i_vmem, o_hbm=o_hbm):
                for k in range(n_picks):
                    pltpu.sync_copy(x_vmem, o_hbm.at[i_vmem.at[k]])

            pltpu.emit_pipeline(
                body,
                grid=(t // SC_WINDOW,),
                in_specs=[pl.BlockSpec((SC_WINDOW, width), lambda i: (i, 0)),
                          pl.BlockSpec((n_picks, SC_WINDOW), lambda i: (0, i))],
                out_specs=[],
                core_axis_name=("core", "subcore"),
                dimension_semantics=(pltpu.PARALLEL,),
            )(x_hbm, i_hbm)

    return scatter(xa, xb, dest)


def _gather_rows(ya, yb, idx):
    n = idx.shape[1]
    width = ya.shape[1]
    out = jax.ShapeDtypeStruct((n, width), ya.dtype)

    @functools.partial(pl.kernel, out_type=[out, out], mesh=_sc_mesh(), scratch_types=[])
    def gather(ya_hbm, yb_hbm, i_hbm, oa_hbm, ob_hbm):
        for y_hbm, o_hbm in ((ya_hbm, oa_hbm), (yb_hbm, ob_hbm)):
            def body(i_vmem, o_vmem, y_hbm=y_hbm):
                pltpu.sync_copy(y_hbm.at[i_vmem.at[0]], o_vmem)

            pltpu.emit_pipeline(
                body,
                grid=(n // SC_WINDOW,),
                in_specs=[pl.BlockSpec((1, SC_WINDOW), lambda i: (0, i))],
                out_specs=[pl.BlockSpec((SC_WINDOW, width), lambda i: (i, 0))],
                core_axis_name=("core", "subcore"),
                dimension_semantics=(pltpu.PARALLEL,),
            )(i_hbm, o_hbm)

    return gather(ya, yb, idx)


def _expert_kernel(first_tile_ref, n_tiles_ref, total_ref, tile_rows_ref, xa_hbm, xb_hbm, w1_ref, w3_ref, w2_ref,
                   ya_hbm, yb_hbm, xa_buf, xb_buf, ya_buf, yb_buf, in_sem, out_sem, w1_bf, w3_bf, w2_bf):
    e = pl.program_id(0)
    tm = EXPERT_TILE
    ring = EXPERT_RING
    total = total_ref[0]
    first_tile = first_tile_ref[e]

    def pieces(g, slot, bufs_hbm, sems, to_vmem):
        out = []
        for r0 in range(0, tm, EXPERT_COPY_ROWS):
            hbm_rows = pl.ds(pl.multiple_of(g * tm + r0, EXPERT_COPY_ROWS), EXPERT_COPY_ROWS)
            for i, (buf, hbm) in enumerate(bufs_hbm):
                vmem = buf.at[slot, pl.ds(r0, EXPERT_COPY_ROWS)]
                src, dst = (hbm.at[hbm_rows], vmem) if to_vmem else (vmem, hbm.at[hbm_rows])
                out.append((r0, pltpu.make_async_copy(src, dst, sems.at[i, slot])))
        return out

    def for_live_pieces(g, bufs_hbm, sems, to_vmem, act):
        slot = g % ring
        live_rows = tile_rows_ref[g]
        for r0, copy in pieces(g, slot, bufs_hbm, sems, to_vmem):
            pl.when(r0 < live_rows)(functools.partial(act, copy))

    ins = ((xa_buf, xa_hbm), (xb_buf, xb_hbm))
    outs = ((ya_buf, ya_hbm), (yb_buf, yb_hbm))
    start_in = lambda g: for_live_pieces(g, ins, in_sem, True, lambda c: c.start())
    wait_in = lambda g: for_live_pieces(g, ins, in_sem, True, lambda c: c.wait())
    start_out = lambda g: for_live_pieces(g, outs, out_sem, False, lambda c: c.start())
    wait_out = lambda g: for_live_pieces(g, outs, out_sem, False, lambda c: c.wait())

    @pl.when(n_tiles_ref[e] > 0)
    def _():
        w1_bf[...] = w1_ref[0].astype(BF16)
        w3_bf[...] = w3_ref[0].astype(BF16)
        w2_bf[...] = w2_ref[0].astype(BF16)

    def tile(t, carry):
        g = first_tile + t
        slot = g % ring

        @pl.when(g == 0)
        def _():
            for ahead in range(ring - 1):
                pl.when(ahead < total)(functools.partial(start_in, ahead))

        pl.when(g + ring - 1 < total)(functools.partial(start_in, g + ring - 1))
        wait_in(g)
        pl.when(g >= ring)(functools.partial(wait_out, g - ring))

        hi, lo = _unpack_bf16_pairs(jnp.concatenate([xa_buf[slot], xb_buf[slot]], axis=1))
        hi = hi.astype(BF16)
        lo = lo.astype(BF16)

        def up(w_bf):
            return (jnp.dot(hi, w_bf[:PACKED, :], preferred_element_type=F32)
                    + jnp.dot(lo, w_bf[PACKED:, :], preferred_element_type=F32))

        act = _silu(up(w1_bf)) * up(w3_bf)
        words = _pack_bf16_pairs(_mm(act, w2_bf[...]))
        ya_buf[slot] = words[:, :HALF]
        yb_buf[slot] = words[:, HALF:]
        start_out(g)
        return carry

    lax.fori_loop(0, n_tiles_ref[e], tile, 0)

    @pl.when(e == pl.num_programs(0) - 1)
    def _():
        for back in range(ring, 0, -1):
            pl.when(total >= back)(functools.partial(wait_out, total - back))


def _experts(xa, xb, first_tile, n_tiles, total_tiles, tile_rows, w1, w3, w2):
    rows = xa.shape[0]
    d, f = D_MODEL, D_EXPERT
    tm = EXPERT_TILE
    weights = lambda a, b: pl.BlockSpec((1, a, b), lambda e, *_: (e, 0, 0))
    ring = lambda: pltpu.VMEM((EXPERT_RING, tm, HALF), I32)
    sems = lambda: pltpu.SemaphoreType.DMA((2, EXPERT_RING))
    return pl.pallas_call(
        _expert_kernel,
        grid_spec=pltpu.PrefetchScalarGridSpec(
            num_scalar_prefetch=4,
            grid=(N_EXPERTS,),
            in_specs=[
                pl.BlockSpec(memory_space=pl.ANY), pl.BlockSpec(memory_space=pl.ANY),
                weights(d, f), weights(d, f), weights(f, d),
            ],
            out_specs=[pl.BlockSpec(memory_space=pl.ANY), pl.BlockSpec(memory_space=pl.ANY)],
            scratch_shapes=[
                ring(), ring(), ring(), ring(), sems(), sems(),
                pltpu.VMEM((d, f), BF16), pltpu.VMEM((d, f), BF16), pltpu.VMEM((f, d), BF16),
            ],
        ),
        out_shape=[jax.ShapeDtypeStruct((rows, HALF), I32), jax.ShapeDtypeStruct((rows, HALF), I32)],
        compiler_params=_params("arbitrary"),
        name="experts",
    )(first_tile, n_tiles, total_tiles, tile_rows, xa, xb, w1, w3, w2)


def _final_kernel(ha_ref, hb_ref, x1_ref, ada_ref, gpost2_ref, w8_ref, yga_ref, ygb_ref, ws1_ref, ws3_ref,
                  ws2_ref, out_ref):
    hi, lo = _unpack_bf16_pairs(jnp.concatenate([ha_ref[...], hb_ref[...]], axis=1))
    hi = hi.astype(BF16)
    lo = lo.astype(BF16)

    def up(w_ref):
        return (jnp.dot(hi, w_ref[:PACKED, :], preferred_element_type=F32)
                + jnp.dot(lo, w_ref[PACKED:, :], preferred_element_type=F32))

    f = _mm(_silu(up(ws1_ref)) * up(ws3_ref), ws2_ref[...])
    for k in range(TOP_K):
        hi, lo = _unpack_bf16_pairs(jnp.concatenate([yga_ref[k], ygb_ref[k]], axis=1))
        f = f + w8_ref[:, k:k + 1] * jnp.concatenate([hi, lo], axis=1)
    out_ref[...] = x1_ref[...] + ada_ref[5, 0] * _rmsnorm(f, gpost2_ref[...])


def _final(ha, hb, x1, ada4, tokens_per_ada_block, w8_t, yga, ygb, first_token, p, tm):
    t, d = x1.shape
    f = D_EXPERT
    first_block = first_token // tm
    tok = lambda: pl.BlockSpec((tm, d), lambda i: (i, 0))
    if ada4.shape[2] == 1:
        ada_spec = pl.BlockSpec((6, 1, 1, d), lambda i: (0, i * tm // tokens_per_ada_block, 0, 0))
    else:
        ada_spec = pl.BlockSpec((6, 1, tm, d), lambda i: (0, 0, i, 0))
    gathered = lambda: pl.BlockSpec((TOP_K, tm, HALF), lambda i: (0, first_block + i, 0))
    return pl.pallas_call(
        _final_kernel,
        grid=(t // tm,),
        in_specs=[
            pl.BlockSpec((tm, HALF), lambda i: (first_block + i, 0)),
            pl.BlockSpec((tm, HALF), lambda i: (first_block + i, 0)),
            tok(), ada_spec, _resident((1, d)),
            pl.BlockSpec((tm, TOP_K), lambda i: (first_block + i, 0)),
            gathered(), gathered(),
            _resident((d, f)), _resident((d, f)), _resident((f, d)),
        ],
        out_specs=tok(),
        out_shape=jax.ShapeDtypeStruct((t, d), F32),
        compiler_params=_params("arbitrary"),
        name="final",
    )(ha, hb, x1, ada4, p["g_post2"], w8_t, yga, ygb, p["ws1"], p["ws3"], p["ws2"])


def kernel(x_prompt, x_sample, state_sconv, state_cconv, c_prompt, c_sample, w_ada, b_ada, g_pre1, g_post1,
           g_pre2, g_post2, w_in, w_sconv, w_out_a, w_cconv, b_cconv, ln_g, ln_b, w_out_b, b_out_b, w_o,
           w_router, e_bias, w1, w3, w2, ws1, ws3, ws2):
    assert w_ada.shape[0] == 1, "single-layer trunk"
    nb, seq, d = x_prompt.shape
    ns = x_sample.shape[0]
    n_prompt = nb * seq
    n_tokens = n_prompt + ns
    p = {
        "g_pre1": g_pre1, "g_post1": g_post1, "g_pre2": g_pre2, "g_post2": g_post2,
        "w_in": w_in[0].astype(BF16), "w_sconv": w_sconv[0], "w_out_a": w_out_a[0].astype(BF16),
        "w_cconv": w_cconv[0], "b_cconv": b_cconv, "ln_g": ln_g, "ln_b": ln_b,
        "w_out_b": w_out_b[0].astype(BF16), "b_out_b": b_out_b, "w_o": w_o[0].astype(BF16),
        "ws1": ws1[0].astype(BF16), "ws3": ws3[0].astype(BF16), "ws2": ws2[0].astype(BF16),
    }
    w_router_t = w_router[0].T.astype(BF16)
    e_bias_col = e_bias[0].reshape(N_EXPERTS, 1)

    ada = _ada(jnp.concatenate([c_prompt, c_sample], axis=0), w_ada[0], b_ada)
    ada = ada.reshape(nb + ns, 6, d).transpose(1, 0, 2)
    ada_p = ada[:, :nb].reshape(6, nb, 1, d)
    ada_s = ada[:, nb:].reshape(6, 1, ns, d)

    x1_p, ha, hb, u_tail, glu_tail = _prompt_mixer(x_prompt, ada_p, n_tokens, p)
    x1_s, ha, hb, new_sconv_sample, new_cconv_sample = _sample_mixer(
        x_sample.reshape(ns, d), ada_s, state_sconv[0], state_cconv[0], ha, hb, n_prompt, p)

    eid, rank, w8_t, cnt = _router(ha, hb, w_router_t, e_bias_col, ROUTER_TILE)
    counts = cnt[:, 0].astype(I32)
    n_tiles = (counts + EXPERT_TILE - 1) // EXPERT_TILE
    tile_ends = jnp.cumsum(n_tiles)
    first_tile = tile_ends - n_tiles
    n_rows = n_tokens * TOP_K + N_EXPERTS * EXPERT_TILE
    dest = _positions(eid, rank, first_tile * EXPERT_TILE)
    tile_ids = jnp.arange(pl.cdiv(n_rows, EXPERT_TILE), dtype=I32)
    tile_owner = jnp.minimum(jnp.sum(tile_ids[:, None] >= tile_ends[None, :], axis=1), N_EXPERTS - 1)
    tile_rows = jnp.clip(counts[tile_owner] - (tile_ids - first_tile[tile_owner]) * EXPERT_TILE, 0, EXPERT_TILE)
    tile_rows = jnp.where(tile_ids < tile_ends[-1], tile_rows, 0).astype(I32)

    xa, xb = _dispatch_rows(ha, hb, dest, n_rows)
    ya, yb = _experts(xa, xb, first_tile, n_tiles, tile_ends[-1:], tile_rows, w1[0], w3[0], w2[0])
    flat = dest.reshape(1, TOP_K * n_tokens)
    yga, ygb = (g.reshape(TOP_K, n_tokens, HALF) for g in _gather_rows(ya, yb, flat))

    y_p = _final(ha, hb, x1_p.reshape(n_prompt, d), ada_p, seq, w8_t, yga, ygb, 0, p, FINAL_TILE)
    y_s = _final(ha, hb, x1_s, ada_s, 1, w8_t, yga, ygb, n_prompt, p, ns)

    new_sconv_prompt = u_tail[:, SUBLANES - (K_SCONV - 1):][None]
    new_cconv_prompt = glu_tail[:, CONV_HALO - (K_CCONV - 1):][None]
    return (y_p.reshape(nb, seq, d), y_s.reshape(ns, 1, d),
            new_sconv_prompt, new_sconv_sample[None], new_cconv_prompt, new_cconv_sample[None])
```

```python
import functools

import jax
import jax.numpy as jnp
from jax import lax
from jax.experimental import pallas as pl
from jax.experimental.pallas import tpu as pltpu
from jax.experimental.pallas import tpu_sc as plsc

D_MODEL = 1024
K_SCONV = 3
K_CCONV = 31
N_EXPERTS = 64
TOP_K = 8
N_GROUPS = 8
GROUP_SIZE = N_EXPERTS // N_GROUPS
TOPK_GROUPS = 4
D_EXPERT = 256
ROUTED_SCALE = 2.5
RMS_EPS = 1e-6
LN_EPS = 1e-5

F32 = jnp.float32
BF16 = jnp.bfloat16
I32 = jnp.int32
NEG_INF = float("-inf")

VMEM_LIMIT_BYTES_V7X = 56 * 1024 * 1024
SUBLANES = 8
LANES = 128

MIXER_TILE = 512
CONV_HALO = 32
CONV_ROW_CHUNK = 64
CONV_LANE_CHUNK = 128
PROJ_LANE_CHUNK = 256
ROUTER_TILE = 384
POSITION_TILE = 5504
EXPERT_TILE = 768
EXPERT_COPY_ROWS = 256
EXPERT_RING = 3
FINAL_TILE = 512
PACKED = D_MODEL // 2
HALF = PACKED // 2
SC_WINDOW = 128


def _params(*semantics):
    return pltpu.CompilerParams(dimension_semantics=semantics, vmem_limit_bytes=VMEM_LIMIT_BYTES_V7X)


def _resident(shape):
    zeros = (0,) * len(shape)
    return pl.BlockSpec(shape, lambda *_: zeros, pipeline_mode=pl.Buffered(1))


def _sigmoid(x):
    return 0.5 * jnp.tanh(0.5 * x) + 0.5


def _silu(x):
    return x * _sigmoid(x)


def _rmsnorm(x, g):
    return x * lax.rsqrt(jnp.mean(x * x, axis=-1, keepdims=True) + RMS_EPS) * g


def _layernorm(x, g, b):
    mu = jnp.mean(x, axis=-1, keepdims=True)
    xc = x - mu
    var = jnp.mean(xc * xc, axis=-1, keepdims=True)
    return xc * lax.rsqrt(var + LN_EPS) * g + b


def _mm(a, w):
    return jnp.dot(a.astype(BF16), w.astype(BF16), preferred_element_type=F32)


def _pack_bf16_pairs(v):
    n = v.shape[1] // 2
    bits = lax.bitcast_convert_type(v.astype(BF16).astype(F32), I32)
    return bits[:, :n] | lax.shift_right_logical(bits[:, n:], 16)


def _unpack_bf16_pairs(words):
    hi = lax.bitcast_convert_type(words & jnp.int32(-65536), F32)
    lo = lax.bitcast_convert_type(lax.shift_left(words, 16), F32)
    return hi, lo


def _ada_kernel(c_ref, w_ref, b_ref, o_ref):
    o_ref[...] = _mm(_silu(c_ref[...]), w_ref[...]) + b_ref[...]


def _ada(c, w_ada, b_ada):
    n = c.shape[0]
    width = w_ada.shape[1]
    bn = D_MODEL
    return pl.pallas_call(
        _ada_kernel,
        grid=(width // bn,),
        in_specs=[
            pl.BlockSpec((n, D_MODEL), lambda i: (0, 0)),
            pl.BlockSpec((D_MODEL, bn), lambda i: (0, i)),
            pl.BlockSpec((1, bn), lambda i: (0, i)),
        ],
        out_specs=pl.BlockSpec((n, bn), lambda i: (0, i)),
        out_shape=jax.ShapeDtypeStruct((n, width), F32),
        compiler_params=_params("arbitrary"),
        name="ada",
    )(c, w_ada, b_ada)


def _mixer_tail(x, ada_ref, y_a, y_b, gate_a, gate_b, w_o_ref, gpost1_ref, gpre2_ref, x1_ref, ha_ref, hb_ref):
    m = gate_a * y_a + gate_b * y_b
    t = _mm(m, w_o_ref[...])
    x1 = x + ada_ref[2, 0] * _rmsnorm(t, gpost1_ref[...])
    h2 = _rmsnorm(x1, gpre2_ref[...]) * (1.0 + ada_ref[4, 0]) + ada_ref[3, 0]
    x1_ref[...] = x1.reshape(x1_ref.shape)
    words = _pack_bf16_pairs(h2)
    ha_ref[...] = words[:, :HALF].reshape(ha_ref.shape)
    hb_ref[...] = words[:, HALF:].reshape(hb_ref.shape)


def _prompt_mixer_kernel(x_ref, ada_ref, gpre1_ref, gpost1_ref, gpre2_ref, w_in_ref, w_sconv_ref,
                         w_out_a_ref, w_cconv_ref, b_cconv_ref, ln_g_ref, ln_b_ref, w_out_b_ref,
                         b_out_b_ref, w_o_ref, ha_init_ref, hb_init_ref,
                         x1_ref, ha_ref, hb_ref, u_tail_ref, glu_tail_ref,
                         u_ext, glu_ext, cv_buf, h_buf, av_buf, ga_buf, gb_buf):
    del ha_init_ref, hb_init_ref
    tm = x_ref.shape[1]
    d = D_MODEL

    @pl.when(pl.program_id(1) == 0)
    def _():
        u_ext[pl.ds(0, SUBLANES), :] = jnp.zeros((SUBLANES, d), F32)
        glu_ext[pl.ds(0, CONV_HALO), :] = jnp.zeros((CONV_HALO, d), F32)

    h_buf[...] = (_rmsnorm(x_ref[0], gpre1_ref[...]) * (1.0 + ada_ref[1, 0]) + ada_ref[0, 0]).astype(BF16)

    def proj(i, lanes=slice(0, d)):
        return jnp.dot(h_buf[...], w_in_ref[:, i * d + lanes.start:i * d + lanes.stop],
                       preferred_element_type=F32)

    glu = proj(3) * _sigmoid(proj(4))
    glu_ext[pl.ds(CONV_HALO, tm), :] = glu
    glu_tail_ref[0] = glu[tm - CONV_HALO:, :]
    first = CONV_HALO - (K_CCONV - 1)

    def conv_piece(r0, lanes):
        acc = None
        for res in range(SUBLANES):
            rows = CONV_ROW_CHUNK + (SUBLANES if res else 0)
            part = None
            for k in range(K_CCONV):
                if (first + k) % SUBLANES != res:
                    continue
                start = r0 + first + k - res
                term = w_cconv_ref[k:k + 1, lanes] * glu_ext[start:start + rows, lanes]
                part = term if part is None else part + term
            part = part[res:res + CONV_ROW_CHUNK, :]
            acc = part if acc is None else acc + part
        cv_buf[r0:r0 + CONV_ROW_CHUNK, lanes] = acc

    def branch_a(lanes):
        u = proj(2, lanes) * proj(0, lanes)
        u_ext[pl.ds(SUBLANES, tm), lanes] = u
        v = (w_sconv_ref[0:1, lanes] * u_ext[pl.ds(SUBLANES - 2, tm), lanes]
             + w_sconv_ref[1:2, lanes] * u_ext[pl.ds(SUBLANES - 1, tm), lanes]
             + w_sconv_ref[2:3, lanes] * u)
        u_tail_ref[0, :, lanes] = u[tm - SUBLANES:, :]
        u_ext[pl.ds(0, SUBLANES), lanes] = u[tm - SUBLANES:, :]
        av_buf[:, lanes] = (proj(1, lanes) * v).astype(BF16)

    def gates(lanes):
        ga_buf[:, lanes] = _sigmoid(proj(5, lanes))
        gb_buf[:, lanes] = _sigmoid(proj(6, lanes))

    for c in range(0, d, PROJ_LANE_CHUNK):
        lanes = slice(c, c + PROJ_LANE_CHUNK)

        @pl.when(pl.program_id(1) + c >= 0)
        def _(lanes=lanes):
            for r0 in range(0, tm, CONV_ROW_CHUNK):
                for l0 in range(lanes.start, lanes.stop, CONV_LANE_CHUNK):
                    conv_piece(r0, slice(l0, l0 + CONV_LANE_CHUNK))
            branch_a(lanes)
            gates(lanes)

    glu_ext[pl.ds(0, CONV_HALO), :] = glu_ext[pl.ds(tm, CONV_HALO), :]

    y_a = jnp.dot(av_buf[...], w_out_a_ref[...], preferred_element_type=F32)
    cv = cv_buf[...] + b_cconv_ref[...]
    y_b = _mm(_silu(_layernorm(cv, ln_g_ref[...], ln_b_ref[...])), w_out_b_ref[...]) + b_out_b_ref[...]

    _mixer_tail(x_ref[0], ada_ref, y_a, y_b, ga_buf[...], gb_buf[...], w_o_ref, gpost1_ref, gpre2_ref,
                x1_ref, ha_ref, hb_ref)


def _prompt_mixer(x, ada4, n_tokens, p):
    nb, seq, d = x.shape
    tm = MIXER_TILE
    row = lambda: _resident((1, d))
    tok = lambda w: pl.BlockSpec((1, tm, w), lambda b, j: (b, j, 0))
    packed = lambda: pl.BlockSpec((tm, HALF), lambda b, j: (b * (seq // tm) + j, 0))
    return pl.pallas_call(
        _prompt_mixer_kernel,
        grid=(nb, seq // tm),
        in_specs=[
            tok(d),
            pl.BlockSpec((6, 1, 1, d), lambda b, j: (0, b, 0, 0)),
            row(), row(), row(),
            _resident(p["w_in"].shape),
            _resident((K_SCONV, d)),
            _resident((d, d)),
            _resident((K_CCONV, d)),
            row(), row(), row(),
            _resident((d, d)),
            row(),
            _resident((d, d)),
            pl.BlockSpec(memory_space=pl.ANY), pl.BlockSpec(memory_space=pl.ANY),
        ],
        out_specs=[
            tok(d), packed(), packed(),
            pl.BlockSpec((1, SUBLANES, d), lambda b, j: (b, 0, 0)),
            pl.BlockSpec((1, CONV_HALO, d), lambda b, j: (b, 0, 0)),
        ],
        out_shape=[
            jax.ShapeDtypeStruct((nb, seq, d), F32),
            jax.ShapeDtypeStruct((n_tokens, HALF), I32),
            jax.ShapeDtypeStruct((n_tokens, HALF), I32),
            jax.ShapeDtypeStruct((nb, SUBLANES, d), F32),
            jax.ShapeDtypeStruct((nb, CONV_HALO, d), F32),
        ],
        scratch_shapes=[
            pltpu.VMEM((tm + SUBLANES, d), F32),
            pltpu.VMEM((tm + CONV_HALO + SUBLANES, d), F32),
            pltpu.VMEM((tm, d), F32),
            pltpu.VMEM((tm, d), BF16),
            pltpu.VMEM((tm, d), BF16),
            pltpu.VMEM((tm, d), F32),
            pltpu.VMEM((tm, d), F32),
        ],
        input_output_aliases={15: 1, 16: 2},
        compiler_params=_params("arbitrary", "arbitrary"),
        name="prompt_mixer",
    )(x, ada4, p["g_pre1"], p["g_post1"], p["g_pre2"], p["w_in"], p["w_sconv"], p["w_out_a"],
      p["w_cconv"], p["b_cconv"], p["ln_g"], p["ln_b"], p["w_out_b"], p["b_out_b"], p["w_o"],
      jnp.zeros((n_tokens, HALF), I32), jnp.zeros((n_tokens, HALF), I32))


def _sample_mixer_kernel(x_ref, ada_ref, st_a_ref, st_b_ref, gpre1_ref, gpost1_ref, gpre2_ref, w_in_ref,
                         w_sconv_ref, w_out_a_ref, w_cconv_ref, b_cconv_ref, ln_g_ref, ln_b_ref,
                         w_out_b_ref, b_out_b_ref, w_o_ref, ha_all_ref, hb_all_ref,
                         x1_ref, ha_ref, hb_ref, new_a_ref, new_b_ref):
    del ha_all_ref, hb_all_ref
    d = D_MODEL
    x = x_ref[...]
    h = (_rmsnorm(x, gpre1_ref[...]) * (1.0 + ada_ref[1, 0]) + ada_ref[0, 0]).astype(BF16)

    def proj(i):
        return jnp.dot(h, w_in_ref[:, i * d:(i + 1) * d], preferred_element_type=F32)

    u = proj(2) * proj(0)
    v = w_sconv_ref[K_SCONV - 1:K_SCONV, :] * u
    for k in range(K_SCONV - 1):
        v = v + w_sconv_ref[k:k + 1, :] * st_a_ref[:, k, :]
    new_a_ref[:, :K_SCONV - 2, :] = st_a_ref[:, 1:, :]
    new_a_ref[:, K_SCONV - 2, :] = u
    y_a = _mm(proj(1) * v, w_out_a_ref[...])

    glu = proj(3) * _sigmoid(proj(4))
    cv = w_cconv_ref[K_CCONV - 1:K_CCONV, :] * glu + b_cconv_ref[...]
    for k in range(K_CCONV - 1):
        cv = cv + w_cconv_ref[k:k + 1, :] * st_b_ref[:, k, :]
    new_b_ref[:, :K_CCONV - 2, :] = st_b_ref[:, 1:, :]
    new_b_ref[:, K_CCONV - 2, :] = glu
    y_b = _mm(_silu(_layernorm(cv, ln_g_ref[...], ln_b_ref[...])), w_out_b_ref[...]) + b_out_b_ref[...]

    _mixer_tail(x, ada_ref, y_a, y_b, _sigmoid(proj(5)), _sigmoid(proj(6)), w_o_ref, gpost1_ref,
                gpre2_ref, x1_ref, ha_ref, hb_ref)


def _sample_mixer(x, ada4, st_a, st_b, ha_all, hb_all, first_token, p):
    n, d = x.shape
    tb = 32
    first_block = first_token // tb
    row = lambda: _resident((1, d))
    tok = lambda w: pl.BlockSpec((tb, w), lambda i: (i, 0))
    state = lambda k: pl.BlockSpec((tb, k - 1, d), lambda i: (i, 0, 0))
    packed = lambda: pl.BlockSpec((tb, HALF), lambda i: (first_block + i, 0))
    return pl.pallas_call(
        _sample_mixer_kernel,
        grid=(n // tb,),
        in_specs=[
            tok(d),
            pl.BlockSpec((6, 1, tb, d), lambda i: (0, 0, i, 0)),
            state(K_SCONV), state(K_CCONV),
            row(), row(), row(),
            _resident(p["w_in"].shape),
            _resident((K_SCONV, d)),
            _resident((d, d)),
            _resident((K_CCONV, d)),
            row(), row(), row(),
            _resident((d, d)),
            row(),
            _resident((d, d)),
            pl.BlockSpec(memory_space=pl.ANY), pl.BlockSpec(memory_space=pl.ANY),
        ],
        out_specs=[tok(d), packed(), packed(), state(K_SCONV), state(K_CCONV)],
        out_shape=[
            jax.ShapeDtypeStruct((n, d), F32),
            jax.ShapeDtypeStruct(ha_all.shape, I32),
            jax.ShapeDtypeStruct(hb_all.shape, I32),
            jax.ShapeDtypeStruct(st_a.shape, F32),
            jax.ShapeDtypeStruct(st_b.shape, F32),
        ],
        input_output_aliases={17: 1, 18: 2},
        compiler_params=_params("arbitrary"),
        name="sample_mixer",
    )(x, ada4, st_a, st_b, p["g_pre1"], p["g_post1"], p["g_pre2"], p["w_in"], p["w_sconv"], p["w_out_a"],
      p["w_cconv"], p["b_cconv"], p["ln_g"], p["ln_b"], p["w_out_b"], p["b_out_b"], p["w_o"], ha_all, hb_all)


def _first_index_of_max(work, iota, axis, limit):
    m = jnp.max(work, axis=axis, keepdims=True)
    return m, jnp.min(jnp.where(work == m, iota, limit), axis=axis, keepdims=True)


def _router_kernel(ha_ref, hb_ref, wr_ref, bias_ref, eid_ref, rank_ref, w8_ref, cnt_ref):
    tm = ha_ref.shape[0]
    hi, lo = _unpack_bf16_pairs(jnp.concatenate([ha_ref[...], hb_ref[...]], axis=1))
    nt = (((1,), (1,)), ((), ()))
    logits = (lax.dot_general(wr_ref[:, :PACKED], hi.astype(BF16), nt, preferred_element_type=F32)
              + lax.dot_general(wr_ref[:, PACKED:], lo.astype(BF16), nt, preferred_element_type=F32))
    scores = _sigmoid(logits)
    sel = scores + bias_ref[...]

    sel3 = sel.reshape(N_GROUPS, GROUP_SIZE, tm)
    io3 = lax.broadcasted_iota(I32, sel3.shape, 1)
    m1, i1 = _first_index_of_max(sel3, io3, 1, GROUP_SIZE)
    m2 = jnp.max(jnp.where(io3 == i1, NEG_INF, sel3), axis=1, keepdims=True)
    gscore = (m1 + m2).reshape(N_GROUPS, tm)

    iog = lax.broadcasted_iota(I32, gscore.shape, 0)
    gmask = jnp.zeros(gscore.shape, jnp.bool_)
    for _ in range(TOPK_GROUPS):
        _, gi = _first_index_of_max(gscore, iog, 0, N_GROUPS)
        pick = iog == gi
        gmask = jnp.logical_or(gmask, pick)
        gscore = jnp.where(pick, NEG_INF, gscore)

    work = jnp.where(gmask.reshape(N_GROUPS, 1, tm), sel3, NEG_INF).reshape(N_EXPERTS, tm)
    ioe = lax.broadcasted_iota(I32, work.shape, 0)
    chosen = jnp.zeros(work.shape, jnp.bool_)
    eids, ws, picks = [], [], []
    for _ in range(TOP_K):
        _, ei = _first_index_of_max(work, ioe, 0, N_EXPERTS)
        pick = ioe == ei
        eids.append(ei)
        picks.append(pick)
        ws.append(jnp.sum(jnp.where(pick, scores, 0.0), axis=0, keepdims=True))
        chosen = jnp.logical_or(chosen, pick)
        work = jnp.where(pick, NEG_INF, work)

    w = jnp.concatenate(ws, axis=0)
    eid_ref[...] = jnp.concatenate(eids, axis=0)
    w8_ref[...] = (w / jnp.sum(w, axis=0, keepdims=True) * ROUTED_SCALE).T

    @pl.when(pl.program_id(0) == 0)
    def _():
        cnt_ref[...] = jnp.zeros(cnt_ref.shape, F32)

    before = lax.broadcasted_iota(I32, (tm, tm), 0) < lax.broadcasted_iota(I32, (tm, tm), 1)
    seen = cnt_ref[:, 0:1] + jnp.dot(chosen.astype(BF16), before.astype(BF16), preferred_element_type=F32)
    rank_ref[...] = jnp.concatenate(
        [jnp.sum(jnp.where(pick, seen, 0.0), axis=0, keepdims=True) for pick in picks], axis=0)
    cnt_ref[...] += jnp.sum(chosen.astype(F32), axis=1, keepdims=True)


def _router(ha, hb, w_router_t, e_bias_col, tm):
    t = ha.shape[0]
    d = D_MODEL
    picks = lambda: pl.BlockSpec((TOP_K, tm), lambda i: (0, i))
    half_rows = lambda: pl.BlockSpec((tm, HALF), lambda i: (i, 0))
    return pl.pallas_call(
        _router_kernel,
        grid=(t // tm,),
        in_specs=[
            half_rows(), half_rows(),
            _resident((N_EXPERTS, d)),
            _resident((N_EXPERTS, 1)),
        ],
        out_specs=[picks(), picks(), pl.BlockSpec((tm, TOP_K), lambda i: (i, 0)),
                   pl.BlockSpec((N_EXPERTS, LANES), lambda i: (0, 0))],
        out_shape=[
            jax.ShapeDtypeStruct((TOP_K, t), I32),
            jax.ShapeDtypeStruct((TOP_K, t), F32),
            jax.ShapeDtypeStruct((t, TOP_K), F32),
            jax.ShapeDtypeStruct((N_EXPERTS, LANES), F32),
        ],
        compiler_params=_params("arbitrary"),
        name="router",
    )(ha, hb, w_router_t, e_bias_col)


def _offset_kernel(off_ref, eid_ref, rank_ref, dest_ref):
    eid = eid_ref[...]
    dest = rank_ref[...].astype(I32)
    for e in range(N_EXPERTS):
        dest = dest + jnp.where(eid == e, off_ref[e], 0)
    dest_ref[...] = dest


def _positions(eid, rank, first_row):
    t = eid.shape[1]
    tm = POSITION_TILE
    picks = lambda: pl.BlockSpec((TOP_K, tm), lambda i, off: (0, i))
    return pl.pallas_call(
        _offset_kernel,
        grid_spec=pltpu.PrefetchScalarGridSpec(
            num_scalar_prefetch=1, grid=(t // tm,), in_specs=[picks(), picks()], out_specs=picks()),
        out_shape=jax.ShapeDtypeStruct((TOP_K, t), I32),
        compiler_params=_params("arbitrary"),
        name="positions",
    )(first_row, eid, rank)


def _sc_mesh():
    return plsc.VectorSubcoreMesh(core_axis_name="core", subcore_axis_name="subcore")


def _dispatch_rows(xa, xb, dest, n_rows):
    t, width = xa.shape
    n_picks = dest.shape[0]
    out = jax.ShapeDtypeStruct((n_rows, width), xa.dtype)

    @functools.partial(pl.kernel, out_type=[out, out], mesh=_sc_mesh(), scratch_types=[])
    def scatter(xa_hbm, xb_hbm, i_hbm, oa_hbm, ob_hbm):
        for x_hbm, o_hbm in ((xa_hbm, oa_hbm), (xb_hbm, ob_hbm)):
            def body(x_vmem, i_vmem, o_hbm=o_hbm):
                for k in range(n_picks):
                    pltpu.sync_copy(x_vmem, o_hbm.at[i_vmem.at[k]])

            pltpu.emit_pipeline(
                body,
                grid=(t // SC_WINDOW,),
                in_specs=[pl.BlockSpec((SC_WINDOW, width), lambda i: (i, 0)),
                          pl.BlockSpec((n_picks, SC_WINDOW), lambda i: (0, i))],
                out_specs=[],
                core_axis_name=("core", "subcore"),
                dimension_semantics=(pltpu.PARALLEL,),
            )(x_hbm, i_hbm)

    return scatter(xa, xb, dest)


def _gather_rows(ya, yb, idx):
    n = idx.shape[1]
    width = ya.shape[1]
    out = jax.ShapeDtypeStruct((n, width), ya.dtype)

    @functools.partial(pl.kernel, out_type=[out, out], mesh=_sc_mesh(), scratch_types=[])
    def gather(ya_hbm, yb_hbm, i_hbm, oa_hbm, ob_hbm):
        for y_hbm, o_hbm in ((ya_hbm, oa_hbm), (yb_hbm, ob_hbm)):
            def body(i_vmem, o_vmem, y_hbm=y_hbm):
                pltpu.sync_copy(y_hbm.at[i_vmem.at[0]], o_vmem)

            pltpu.emit_pipeline(
                body,
                grid=(n // SC_WINDOW,),
                in_specs=[pl.BlockSpec((1, SC_WINDOW), lambda i: (0, i))],
                out_specs=[pl.BlockSpec((SC_WINDOW, width), lambda i: (i, 0))],
                core_axis_name=("core", "subcore"),
                dimension_semantics=(pltpu.PARALLEL,),
            )(i_hbm, o_hbm)

    return gather(ya, yb, idx)


def _expert_kernel(first_tile_ref, n_tiles_ref, total_ref, tile_rows_ref, xa_hbm, xb_hbm, w1_ref, w3_ref, w2_ref,
                   ya_hbm, yb_hbm, xa_buf, xb_buf, ya_buf, yb_buf, in_sem, out_sem, w1_bf, w3_bf, w2_bf):
    e = pl.program_id(0)
    tm = EXPERT_TILE
    ring = EXPERT_RING
    total = total_ref[0]
    first_tile = first_tile_ref[e]

    def pieces(g, slot, bufs_hbm, sems, to_vmem):
        out = []
        for r0 in range(0, tm, EXPERT_COPY_ROWS):
            hbm_rows = pl.ds(pl.multiple_of(g * tm + r0, EXPERT_COPY_ROWS), EXPERT_COPY_ROWS)
            for i, (buf, hbm) in enumerate(bufs_hbm):
                vmem = buf.at[slot, pl.ds(r0, EXPERT_COPY_ROWS)]
                src, dst = (hbm.at[hbm_rows], vmem) if to_vmem else (vmem, hbm.at[hbm_rows])
                out.append((r0, pltpu.make_async_copy(src, dst, sems.at[i, slot])))
        return out

    def for_live_pieces(g, bufs_hbm, sems, to_vmem, act):
        slot = g % ring
        live_rows = tile_rows_ref[g]
        for r0, copy in pieces(g, slot, bufs_hbm, sems, to_vmem):
            pl.when(r0 < live_rows)(functools.partial(act, copy))

    ins = ((xa_buf, xa_hbm), (xb_buf, xb_hbm))
    outs = ((ya_buf, ya_hbm), (yb_buf, yb_hbm))
    start_in = lambda g: for_live_pieces(g, ins, in_sem, True, lambda c: c.start())
    wait_in = lambda g: for_live_pieces(g, ins, in_sem, True, lambda c: c.wait())
    start_out = lambda g: for_live_pieces(g, outs, out_sem, False, lambda c: c.start())
    wait_out = lambda g: for_live_pieces(g, outs, out_sem, False, lambda c: c.wait())

    @pl.when(n_tiles_ref[e] > 0)
    def _():
        w1_bf[...] = w1_ref[0].astype(BF16)
        w3_bf[...] = w3_ref[0].astype(BF16)
        w2_bf[...] = w2_ref[0].astype(BF16)

    def tile(t, carry):
        g = first_tile + t
        slot = g % ring

        @pl.when(g == 0)
        def _():
            for ahead in range(ring - 1):
                pl.when(ahead < total)(functools.partial(start_in, ahead))

        pl.when(g + ring - 1 < total)(functools.partial(start_in, g + ring - 1))
        wait_in(g)
        pl.when(g >= ring)(functools.partial(wait_out, g - ring))

        hi, lo = _unpack_bf16_pairs(jnp.concatenate([xa_buf[slot], xb_buf[slot]], axis=1))
        hi = hi.astype(BF16)
        lo = lo.astype(BF16)

        def up(w_bf):
            return (jnp.dot(hi, w_bf[:PACKED, :], preferred_element_type=F32)
                    + jnp.dot(lo, w_bf[PACKED:, :], preferred_element_type=F32))

        act = _silu(up(w1_bf)) * up(w3_bf)
        words = _pack_bf16_pairs(_mm(act, w2_bf[...]))
        ya_buf[slot] = words[:, :HALF]
        yb_buf[slot] = words[:, HALF:]
        start_out(g)
        return carry

    lax.fori_loop(0, n_tiles_ref[e], tile, 0)

    @pl.when(e == pl.num_programs(0) - 1)
    def _():
        for back in range(ring, 0, -1):
            pl.when(total >= back)(functools.partial(wait_out, total - back))


def _experts(xa, xb, first_tile, n_tiles, total_tiles, tile_rows, w1, w3, w2):
    rows = xa.shape[0]
    d, f = D_MODEL, D_EXPERT
    tm = EXPERT_TILE
    weights = lambda a, b: pl.BlockSpec((1, a, b), lambda e, *_: (e, 0, 0))
    ring = lambda: pltpu.VMEM((EXPERT_RING, tm, HALF), I32)
    sems = lambda: pltpu.SemaphoreType.DMA((2, EXPERT_RING))
    return pl.pallas_call(
        _expert_kernel,
        grid_spec=pltpu.PrefetchScalarGridSpec(
            num_scalar_prefetch=4,
            grid=(N_EXPERTS,),
            in_specs=[
                pl.BlockSpec(memory_space=pl.ANY), pl.BlockSpec(memory_space=pl.ANY),
                weights(d, f), weights(d, f), weights(f, d),
            ],
            out_specs=[pl.BlockSpec(memory_space=pl.ANY), pl.BlockSpec(memory_space=pl.ANY)],
            scratch_shapes=[
                ring(), ring(), ring(), ring(), sems(), sems(),
                pltpu.VMEM((d, f), BF16), pltpu.VMEM((d, f), BF16), pltpu.VMEM((f, d), BF16),
            ],
        ),
        out_shape=[jax.ShapeDtypeStruct((rows, HALF), I32), jax.ShapeDtypeStruct((rows, HALF), I32)],
        compiler_params=_params("arbitrary"),
        name="experts",
    )(first_tile, n_tiles, total_tiles, tile_rows, xa, xb, w1, w3, w2)


def _final_kernel(ha_ref, hb_ref, x1_ref, ada_ref, gpost2_ref, w8_ref, yga_ref, ygb_ref, ws1_ref, ws3_ref,
                  ws2_ref, out_ref):
    hi, lo = _unpack_bf16_pairs(jnp.concatenate([ha_ref[...], hb_ref[...]], axis=1))
    hi = hi.astype(BF16)
    lo = lo.astype(BF16)

    def up(w_ref):
        return (jnp.dot(hi, w_ref[:PACKED, :], preferred_element_type=F32)
                + jnp.dot(lo, w_ref[PACKED:, :], preferred_element_type=F32))

    f = _mm(_silu(up(ws1_ref)) * up(ws3_ref), ws2_ref[...])
    for k in range(TOP_K):
        hi, lo = _unpack_bf16_pairs(jnp.concatenate([yga_ref[k], ygb_ref[k]], axis=1))
        f = f + w8_ref[:, k:k + 1] * jnp.concatenate([hi, lo], axis=1)
    out_ref[...] = x1_ref[...] + ada_ref[5, 0] * _rmsnorm(f, gpost2_ref[...])


def _final(ha, hb, x1, ada4, tokens_per_ada_block, w8_t, yga, ygb, first_token, p, tm):
    t, d = x1.shape
    f = D_EXPERT
    first_block = first_token // tm
    tok = lambda: pl.BlockSpec((tm, d), lambda i: (i, 0))
    if ada4.shape[2] == 1:
        ada_spec = pl.BlockSpec((6, 1, 1, d), lambda i: (0, i * tm // tokens_per_ada_block, 0, 0))
    else:
        ada_spec = pl.BlockSpec((6, 1, tm, d), lambda i: (0, 0, i, 0))
    gathered = lambda: pl.BlockSpec((TOP_K, tm, HALF), lambda i: (0, first_block + i, 0))
    return pl.pallas_call(
        _final_kernel,
        grid=(t // tm,),
        in_specs=[
            pl.BlockSpec((tm, HALF), lambda i: (first_block + i, 0)),
            pl.BlockSpec((tm, HALF), lambda i: (first_block + i, 0)),
            tok(), ada_spec, _resident((1, d)),
            pl.BlockSpec((tm, TOP_K), lambda i: (first_block + i, 0)),
            gathered(), gathered(),
            _resident((d, f)), _resident((d, f)), _resident((f, d)),
        ],
        out_specs=tok(),
        out_shape=jax.ShapeDtypeStruct((t, d), F32),
        compiler_params=_params("arbitrary"),
        name="final",
    )(ha, hb, x1, ada4, p["g_post2"], w8_t, yga, ygb, p["ws1"], p["ws3"], p["ws2"])


def kernel(x_prompt, x_sample, state_sconv, state_cconv, c_prompt, c_sample, w_ada, b_ada, g_pre1, g_post1,
           g_pre2, g_post2, w_in, w_sconv, w_out_a, w_cconv, b_cconv, ln_g, ln_b, w_out_b, b_out_b, w_o,
           w_router, e_bias, w1, w3, w2, ws1, ws3, ws2):
    assert w_ada.shape[0] == 1, "single-layer trunk"
    nb, seq, d = x_prompt.shape
    ns = x_sample.shape[0]
    n_prompt = nb * seq
    n_tokens = n_prompt + ns
    p = {
        "g_pre1": g_pre1, "g_post1": g_post1, "g_pre2": g_pre2, "g_post2": g_post2,
        "w_in": w_in[0].astype(BF16), "w_sconv": w_sconv[0], "w_out_a": w_out_a[0].astype(BF16),
        "w_cconv": w_cconv[0], "b_cconv": b_cconv, "ln_g": ln_g, "ln_b": ln_b,
        "w_out_b": w_out_b[0].astype(BF16), "b_out_b": b_out_b, "w_o": w_o[0].astype(BF16),
        "ws1": ws1[0].astype(BF16), "ws3": ws3[0].astype(BF16), "ws2": ws2[0].astype(BF16),
    }
    w_router_t = w_router[0].T.astype(BF16)
    e_bias_col = e_bias[0].reshape(N_EXPERTS, 1)

    ada = _ada(jnp.concatenate([c_prompt, c_sample], axis=0), w_ada[0], b_ada)
    ada = ada.reshape(nb + ns, 6, d).transpose(1, 0, 2)
    ada_p = ada[:, :nb].reshape(6, nb, 1, d)
    ada_s = ada[:, nb:].reshape(6, 1, ns, d)

    x1_p, ha, hb, u_tail, glu_tail = _prompt_mixer(x_prompt, ada_p, n_tokens, p)
    x1_s, ha, hb, new_sconv_sample, new_cconv_sample = _sample_mixer(
        x_sample.reshape(ns, d), ada_s, state_sconv[0], state_cconv[0], ha, hb, n_prompt, p)

    eid, rank, w8_t, cnt = _router(ha, hb, w_router_t, e_bias_col, ROUTER_TILE)
    counts = cnt[:, 0].astype(I32)
    n_tiles = (counts + EXPERT_TILE - 1) // EXPERT_TILE
    tile_ends = jnp.cumsum(n_tiles)
    first_tile = tile_ends - n_tiles
    n_rows = n_tokens * TOP_K + N_EXPERTS * EXPERT_TILE
    dest = _positions(eid, rank, first_tile * EXPERT_TILE)
    tile_ids = jnp.arange(pl.cdiv(n_rows, EXPERT_TILE), dtype=I32)
    tile_owner = jnp.minimum(jnp.sum(tile_ids[:, None] >= tile_ends[None, :], axis=1), N_EXPERTS - 1)
    tile_rows = jnp.clip(counts[tile_owner] - (tile_ids - first_tile[tile_owner]) * EXPERT_TILE, 0, EXPERT_TILE)
    tile_rows = jnp.where(tile_ids < tile_ends[-1], tile_rows, 0).astype(I32)

    xa, xb = _dispatch_rows(ha, hb, dest, n_rows)
    ya, yb = _experts(xa, xb, first_tile, n_tiles, tile_ends[-1:], tile_rows, w1[0], w3[0], w2[0])
    flat = dest.reshape(1, TOP_K * n_tokens)
    yga, ygb = (g.reshape(TOP_K, n_tokens, HALF) for g in _gather_rows(ya, yb, flat))

    y_p = _final(ha, hb, x1_p.reshape(n_prompt, d), ada_p, seq, w8_t, yga, ygb, 0, p, FINAL_TILE)
    y_s = _final(ha, hb, x1_s, ada_s, 1, w8_t, yga, ygb, n_prompt, p, ns)

    new_sconv_prompt = u_tail[:, SUBLANES - (K_SCONV - 1):][None]
    new_cconv_prompt = glu_tail[:, CONV_HALO - (K_CCONV - 1):][None]
    return (y_p.reshape(nb, seq, d), y_s.reshape(ns, 1, d),
            new_sconv_prompt, new_sconv_sample[None], new_cconv_prompt, new_cconv_sample[None])
```

```python
import functools

import jax
import jax.numpy as jnp
from jax import lax
from jax.experimental import pallas as pl
from jax.experimental.pallas import tpu as pltpu
from jax.experimental.pallas import tpu_sc as plsc

D_MODEL = 1024
K_SCONV = 3
K_CCONV = 31
N_EXPERTS = 64
TOP_K = 8
N_GROUPS = 8
GROUP_SIZE = N_EXPERTS // N_GROUPS
TOPK_GROUPS = 4
D_EXPERT = 256
ROUTED_SCALE = 2.5
RMS_EPS = 1e-6
LN_EPS = 1e-5

F32 = jnp.float32
BF16 = jnp.bfloat16
I32 = jnp.int32
NEG_INF = float("-inf")

VMEM_LIMIT_BYTES_V7X = 56 * 1024 * 1024
SUBLANES = 8
LANES = 128

MIXER_TILE = 512
CONV_HALO = 32
CONV_ROW_CHUNK = 64
CONV_LANE_CHUNK = 128
PROJ_LANE_CHUNK = 256
ROUTER_TILE = 384
POSITION_TILE = 5504
EXPERT_TILE = 768
EXPERT_COPY_ROWS = 256
EXPERT_RING = 3
FINAL_TILE = 512
PACKED = D_MODEL // 2
HALF = PACKED // 2
SC_WINDOW = 128


def _params(*semantics):
    return pltpu.CompilerParams(dimension_semantics=semantics, vmem_limit_bytes=VMEM_LIMIT_BYTES_V7X)


def _resident(shape):
    zeros = (0,) * len(shape)
    return pl.BlockSpec(shape, lambda *_: zeros, pipeline_mode=pl.Buffered(1))


def _sigmoid(x):
    return 0.5 * jnp.tanh(0.5 * x) + 0.5


def _silu(x):
    return x * _sigmoid(x)


def _rmsnorm(x, g):
    return x * lax.rsqrt(jnp.mean(x * x, axis=-1, keepdims=True) + RMS_EPS) * g


def _layernorm(x, g, b):
    mu = jnp.mean(x, axis=-1, keepdims=True)
    xc = x - mu
    var = jnp.mean(xc * xc, axis=-1, keepdims=True)
    return xc * lax.rsqrt(var + LN_EPS) * g + b


def _mm(a, w):
    return jnp.dot(a.astype(BF16), w.astype(BF16), preferred_element_type=F32)


def _pack_bf16_pairs(v):
    n = v.shape[1] // 2
    bits = lax.bitcast_convert_type(v.astype(BF16).astype(F32), I32)
    return bits[:, :n] | lax.shift_right_logical(bits[:, n:], 16)


def _unpack_bf16_pairs(words):
    hi = lax.bitcast_convert_type(words & jnp.int32(-65536), F32)
    lo = lax.bitcast_convert_type(lax.shift_left(words, 16), F32)
    return hi, lo


def _ada_kernel(c_ref, w_ref, b_ref, o_ref):
    o_ref[...] = _mm(_silu(c_ref[...]), w_ref[...]) + b_ref[...]


def _ada(c, w_ada, b_ada):
    n = c.shape[0]
    width = w_ada.shape[1]
    bn = D_MODEL
    return pl.pallas_call(
        _ada_kernel,
        grid=(width // bn,),
        in_specs=[
            pl.BlockSpec((n, D_MODEL), lambda i: (0, 0)),
            pl.BlockSpec((D_MODEL, bn), lambda i: (0, i)),
            pl.BlockSpec((1, bn), lambda i: (0, i)),
        ],
        out_specs=pl.BlockSpec((n, bn), lambda i: (0, i)),
        out_shape=jax.ShapeDtypeStruct((n, width), F32),
        compiler_params=_params("arbitrary"),
        name="ada",
    )(c, w_ada, b_ada)


def _mixer_tail(x, ada_ref, y_a, y_b, gate_a, gate_b, w_o_ref, gpost1_ref, gpre2_ref, x1_ref, ha_ref, hb_ref):
    m = gate_a * y_a + gate_b * y_b
    t = _mm(m, w_o_ref[...])
    x1 = x + ada_ref[2, 0] * _rmsnorm(t, gpost1_ref[...])
    h2 = _rmsnorm(x1, gpre2_ref[...]) * (1.0 + ada_ref[4, 0]) + ada_ref[3, 0]
    x1_ref[...] = x1.reshape(x1_ref.shape)
    words = _pack_bf16_pairs(h2)
    ha_ref[...] = words[:, :HALF].reshape(ha_ref.shape)
    hb_ref[...] = words[:, HALF:].reshape(hb_ref.shape)


def _prompt_mixer_kernel(x_ref, ada_ref, gpre1_ref, gpost1_ref, gpre2_ref, w_in_ref, w_sconv_ref,
                         w_out_a_ref, w_cconv_ref, b_cconv_ref, ln_g_ref, ln_b_ref, w_out_b_ref,
                         b_out_b_ref, w_o_ref, ha_init_ref, hb_init_ref,
                         x1_ref, ha_ref, hb_ref, u_tail_ref, glu_tail_ref,
                         u_ext, glu_ext, cv_buf, h_buf, av_buf, ga_buf, gb_buf):
    del ha_init_ref, hb_init_ref
    tm = x_ref.shape[1]
    d = D_MODEL

    @pl.when(pl.program_id(1) == 0)
    def _():
        u_ext[pl.ds(0, SUBLANES), :] = jnp.zeros((SUBLANES, d), F32)
        glu_ext[pl.ds(0, CONV_HALO), :] = jnp.zeros((CONV_HALO, d), F32)

    h_buf[...] = (_rmsnorm(x_ref[0], gpre1_ref[...]) * (1.0 + ada_ref[1, 0]) + ada_ref[0, 0]).astype(BF16)

    def proj(i, lanes=slice(0, d)):
        return jnp.dot(h_buf[...], w_in_ref[:, i * d + lanes.start:i * d + lanes.stop],
                       preferred_element_type=F32)

    glu = proj(3) * _sigmoid(proj(4))
    glu_ext[pl.ds(CONV_HALO, tm), :] = glu
    glu_tail_ref[0] = glu[tm - CONV_HALO:, :]
    first = CONV_HALO - (K_CCONV - 1)

    def conv_piece(r0, lanes):
        acc = None
        for res in range(SUBLANES):
            rows = CONV_ROW_CHUNK + (SUBLANES if res else 0)
            part = None
            for k in range(K_CCONV):
                if (first + k) % SUBLANES != res:
                    continue
                start = r0 + first + k - res
                term = w_cconv_ref[k:k + 1, lanes] * glu_ext[start:start + rows, lanes]
                part = term if part is None else part + term
            part = part[res:res + CONV_ROW_CHUNK, :]
            acc = part if acc is None else acc + part
        cv_buf[r0:r0 + CONV_ROW_CHUNK, lanes] = acc

    def branch_a(lanes):
        u = proj(2, lanes) * proj(0, lanes)
        u_ext[pl.ds(SUBLANES, tm), lanes] = u
        v = (w_sconv_ref[0:1, lanes] * u_ext[pl.ds(SUBLANES - 2, tm), lanes]
             + w_sconv_ref[1:2, lanes] * u_ext[pl.ds(SUBLANES - 1, tm), lanes]
             + w_sconv_ref[2:3, lanes] * u)
        u_tail_ref[0, :, lanes] = u[tm - SUBLANES:, :]
        u_ext[pl.ds(0, SUBLANES), lanes] = u[tm - SUBLANES:, :]
        av_buf[:, lanes] = (proj(1, lanes) * v).astype(BF16)

    def gates(lanes):
        ga_buf[:, lanes] = _sigmoid(proj(5, lanes))
        gb_buf[:, lanes] = _sigmoid(proj(6, lanes))

    for c in range(0, d, PROJ_LANE_CHUNK):
        lanes = slice(c, c + PROJ_LANE_CHUNK)

        @pl.when(pl.program_id(1) + c >= 0)
        def _(lanes=lanes):
            for r0 in range(0, tm, CONV_ROW_CHUNK):
                for l0 in range(lanes.start, lanes.stop, CONV_LANE_CHUNK):
                    conv_piece(r0, slice(l0, l0 + CONV_LANE_CHUNK))
            branch_a(lanes)
            gates(lanes)

    glu_ext[pl.ds(0, CONV_HALO), :] = glu_ext[pl.ds(tm, CONV_HALO), :]

    y_a = jnp.dot(av_buf[...], w_out_a_ref[...], preferred_element_type=F32)
    cv = cv_buf[...] + b_cconv_ref[...]
    y_b = _mm(_silu(_layernorm(cv, ln_g_ref[...], ln_b_ref[...])), w_out_b_ref[...]) + b_out_b_ref[...]

    _mixer_tail(x_ref[0], ada_ref, y_a, y_b, ga_buf[...], gb_buf[...], w_o_ref, gpost1_ref, gpre2_ref,
                x1_ref, ha_ref, hb_ref)


def _prompt_mixer(x, ada4, n_tokens, p):
    nb, seq, d = x.shape
    tm = MIXER_TILE
    row = lambda: _resident((1, d))
    tok = lambda w: pl.BlockSpec((1, tm, w), lambda b, j: (b, j, 0))
    packed = lambda: pl.BlockSpec((tm, HALF), lambda b, j: (b * (seq // tm) + j, 0))
    return pl.pallas_call(
        _prompt_mixer_kernel,
        grid=(nb, seq // tm),
        in_specs=[
            tok(d),
            pl.BlockSpec((6, 1, 1, d), lambda b, j: (0, b, 0, 0)),
            row(), row(), row(),
            _resident(p["w_in"].shape),
            _resident((K_SCONV, d)),
            _resident((d, d)),
            _resident((K_CCONV, d)),
            row(), row(), row(),
            _resident((d, d)),
            row(),
            _resident((d, d)),
            pl.BlockSpec(memory_space=pl.ANY), pl.BlockSpec(memory_space=pl.ANY),
        ],
        out_specs=[
            tok(d), packed(), packed(),
            pl.BlockSpec((1, SUBLANES, d), lambda b, j: (b, 0, 0)),
            pl.BlockSpec((1, CONV_HALO, d), lambda b, j: (b, 0, 0)),
        ],
        out_shape=[
            jax.ShapeDtypeStruct((nb, seq, d), F32),
            jax.ShapeDtypeStruct((n_tokens, HALF), I32),
            jax.ShapeDtypeStruct((n_tokens, HALF), I32),
            jax.ShapeDtypeStruct((nb, SUBLANES, d), F32),
            jax.ShapeDtypeStruct((nb, CONV_HALO, d), F32),
        ],
        scratch_shapes=[
            pltpu.VMEM((tm + SUBLANES, d), F32),
            pltpu.VMEM((tm + CONV_HALO + SUBLANES, d), F32),
            pltpu.VMEM((tm, d), F32),
            pltpu.VMEM((tm, d), BF16),
            pltpu.VMEM((tm, d), BF16),
            pltpu.VMEM((tm, d), F32),
            pltpu.VMEM((tm, d), F32),
        ],
        input_output_aliases={15: 1, 16: 2},
        compiler_params=_params("arbitrary", "arbitrary"),
        name="prompt_mixer",
    )(x, ada4, p["g_pre1"], p["g_post1"], p["g_pre2"], p["w_in"], p["w_sconv"], p["w_out_a"],
      p["w_cconv"], p["b_cconv"], p["ln_g"], p["ln_b"], p["w_out_b"], p["b_out_b"], p["w_o"],
      jnp.zeros((n_tokens, HALF), I32), jnp.zeros((n_tokens, HALF), I32))


def _sample_mixer_kernel(x_ref, ada_ref, st_a_ref, st_b_ref, gpre1_ref, gpost1_ref, gpre2_ref, w_in_ref,
                         w_sconv_ref, w_out_a_ref, w_cconv_ref, b_cconv_ref, ln_g_ref, ln_b_ref,
                         w_out_b_ref, b_out_b_ref, w_o_ref, ha_all_ref, hb_all_ref,
                         x1_ref, ha_ref, hb_ref, new_a_ref, new_b_ref):
    del ha_all_ref, hb_all_ref
    d = D_MODEL
    x = x_ref[...]
    h = (_rmsnorm(x, gpre1_ref[...]) * (1.0 + ada_ref[1, 0]) + ada_ref[0, 0]).astype(BF16)

    def proj(i):
        return jnp.dot(h, w_in_ref[:, i * d:(i + 1) * d], preferred_element_type=F32)

    u = proj(2) * proj(0)
    v = w_sconv_ref[K_SCONV - 1:K_SCONV, :] * u
    for k in range(K_SCONV - 1):
        v = v + w_sconv_ref[k:k + 1, :] * st_a_ref[:, k, :]
    new_a_ref[:, :K_SCONV - 2, :] = st_a_ref[:, 1:, :]
    new_a_ref[:, K_SCONV - 2, :] = u
    y_a = _mm(proj(1) * v, w_out_a_ref[...])

    glu = proj(3) * _sigmoid(proj(4))
    cv = w_cconv_ref[K_CCONV - 1:K_CCONV, :] * glu + b_cconv_ref[...]
    for k in range(K_CCONV - 1):
        cv = cv + w_cconv_ref[k:k + 1, :] * st_b_ref[:, k, :]
    new_b_ref[:, :K_CCONV - 2, :] = st_b_ref[:, 1:, :]
    new_b_ref[:, K_CCONV - 2, :] = glu
    y_b = _mm(_silu(_layernorm(cv, ln_g_ref[...], ln_b_ref[...])), w_out_b_ref[...]) + b_out_b_ref[...]

    _mixer_tail(x, ada_ref, y_a, y_b, _sigmoid(proj(5)), _sigmoid(proj(6)), w_o_ref, gpost1_ref,
                gpre2_ref, x1_ref, ha_ref, hb_ref)


def _sample_mixer(x, ada4, st_a, st_b, ha_all, hb_all, first_token, p):
    n, d = x.shape
    tb = 32
    first_block = first_token // tb
    row = lambda: _resident((1, d))
    tok = lambda w: pl.BlockSpec((tb, w), lambda i: (i, 0))
    state = lambda k: pl.BlockSpec((tb, k - 1, d), lambda i: (i, 0, 0))
    packed = lambda: pl.BlockSpec((tb, HALF), lambda i: (first_block + i, 0))
    return pl.pallas_call(
        _sample_mixer_kernel,
        grid=(n // tb,),
        in_specs=[
            tok(d),
            pl.BlockSpec((6, 1, tb, d), lambda i: (0, 0, i, 0)),
            state(K_SCONV), state(K_CCONV),
            row(), row(), row(),
            _resident(p["w_in"].shape),
            _resident((K_SCONV, d)),
            _resident((d, d)),
            _resident((K_CCONV, d)),
            row(), row(), row(),
            _resident((d, d)),
            row(),
            _resident((d, d)),
            pl.BlockSpec(memory_space=pl.ANY), pl.BlockSpec(memory_space=pl.ANY),
        ],
        out_specs=[tok(d), packed(), packed(), state(K_SCONV), state(K_CCONV)],
        out_shape=[
            jax.ShapeDtypeStruct((n, d), F32),
            jax.ShapeDtypeStruct(ha_all.shape, I32),
            jax.ShapeDtypeStruct(hb_all.shape, I32),
            jax.ShapeDtypeStruct(st_a.shape, F32),
            jax.ShapeDtypeStruct(st_b.shape, F32),
        ],
        input_output_aliases={17: 1, 18: 2},
        compiler_params=_params("arbitrary"),
        name="sample_mixer",
    )(x, ada4, st_a, st_b, p["g_pre1"], p["g_post1"], p["g_pre2"], p["w_in"], p["w_sconv"], p["w_out_a"],
      p["w_cconv"], p["b_cconv"], p["ln_g"], p["ln_b"], p["w_out_b"], p["b_out_b"], p["w_o"], ha_all, hb_all)


def _first_index_of_max(work, iota, axis, limit):
    m = jnp.max(work, axis=axis, keepdims=True)
    return m, jnp.min(jnp.where(work == m, iota, limit), axis=axis, keepdims=True)


def _router_kernel(ha_ref, hb_ref, wr_ref, bias_ref, eid_ref, rank_ref, w8_ref, cnt_ref):
    tm = ha_ref.shape[0]
    hi, lo = _unpack_bf16_pairs(jnp.concatenate([ha_ref[...], hb_ref[...]], axis=1))
    nt = (((1,), (1,)), ((), ()))
    logits = (lax.dot_general(wr_ref[:, :PACKED], hi.astype(BF16), nt, preferred_element_type=F32)
              + lax.dot_general(wr_ref[:, PACKED:], lo.astype(BF16), nt, preferred_element_type=F32))
    scores = _sigmoid(logits)
    sel = scores + bias_ref[...]

    sel3 = sel.reshape(N_GROUPS, GROUP_SIZE, tm)
    io3 = lax.broadcasted_iota(I32, sel3.shape, 1)
    m1, i1 = _first_index_of_max(sel3, io3, 1, GROUP_SIZE)
    m2 = jnp.max(jnp.where(io3 == i1, NEG_INF, sel3), axis=1, keepdims=True)
    gscore = (m1 + m2).reshape(N_GROUPS, tm)

    iog = lax.broadcasted_iota(I32, gscore.shape, 0)
    gmask = jnp.zeros(gscore.shape, jnp.bool_)
    for _ in range(TOPK_GROUPS):
        _, gi = _first_index_of_max(gscore, iog, 0, N_GROUPS)
        pick = iog == gi
        gmask = jnp.logical_or(gmask, pick)
        gscore = jnp.where(pick, NEG_INF, gscore)

    work = jnp.where(gmask.reshape(N_GROUPS, 1, tm), sel3, NEG_INF).reshape(N_EXPERTS, tm)
    ioe = lax.broadcasted_iota(I32, work.shape, 0)
    chosen = jnp.zeros(work.shape, jnp.bool_)
    eids, ws, picks = [], [], []
    for _ in range(TOP_K):
        _, ei = _first_index_of_max(work, ioe, 0, N_EXPERTS)
        pick = ioe == ei
        eids.append(ei)
        picks.append(pick)
        ws.append(jnp.sum(jnp.where(pick, scores, 0.0), axis=0, keepdims=True))
        chosen = jnp.logical_or(chosen, pick)
        work = jnp.where(pick, NEG_INF, work)

    w = jnp.concatenate(ws, axis=0)
    eid_ref[...] = jnp.concatenate(eids, axis=0)
    w8_ref[...] = (w / jnp.sum(w, axis=0, keepdims=True) * ROUTED_SCALE).T

    @pl.when(pl.program_id(0) == 0)
    def _():
        cnt_ref[...] = jnp.zeros(cnt_ref.shape, F32)

    before = lax.broadcasted_iota(I32, (tm, tm), 0) < lax.broadcasted_iota(I32, (tm, tm), 1)
    seen = cnt_ref[:, 0:1] + jnp.dot(chosen.astype(BF16), before.astype(BF16), preferred_element_type=F32)
    rank_ref[...] = jnp.concatenate(
        [jnp.sum(jnp.where(pick, seen, 0.0), axis=0, keepdims=True) for pick in picks], axis=0)
    cnt_ref[...] += jnp.sum(chosen.astype(F32), axis=1, keepdims=True)


def _router(ha, hb, w_router_t, e_bias_col, tm):
    t = ha.shape[0]
    d = D_MODEL
    picks = lambda: pl.BlockSpec((TOP_K, tm), lambda i: (0, i))
    half_rows = lambda: pl.BlockSpec((tm, HALF), lambda i: (i, 0))
    return pl.pallas_call(
        _router_kernel,
        grid=(t // tm,),
        in_specs=[
            half_rows(), half_rows(),
            _resident((N_EXPERTS, d)),
            _resident((N_EXPERTS, 1)),
        ],
        out_specs=[picks(), picks(), pl.BlockSpec((tm, TOP_K), lambda i: (i, 0)),
                   pl.BlockSpec((N_EXPERTS, LANES), lambda i: (0, 0))],
        out_shape=[
            jax.ShapeDtypeStruct((TOP_K, t), I32),
            jax.ShapeDtypeStruct((TOP_K, t), F32),
            jax.ShapeDtypeStruct((t, TOP_K), F32),
            jax.ShapeDtypeStruct((N_EXPERTS, LANES), F32),
        ],
        compiler_params=_params("arbitrary"),
        name="router",
    )(ha, hb, w_router_t, e_bias_col)


def _offset_kernel(off_ref, eid_ref, rank_ref, dest_ref):
    eid = eid_ref[...]
    dest = rank_ref[...].astype(I32)
    for e in range(N_EXPERTS):
        dest = dest + jnp.where(eid == e, off_ref[e], 0)
    dest_ref[...] = dest


def _positions(eid, rank, first_row):
    t = eid.shape[1]
    tm = POSITION_TILE
    picks = lambda: pl.BlockSpec((TOP_K, tm), lambda i, off: (0, i))
    return pl.pallas_call(
        _offset_kernel,
        grid_spec=pltpu.PrefetchScalarGridSpec(
            num_scalar_prefetch=1, grid=(t // tm,), in_specs=[picks(), picks()], out_specs=picks()),
        out_shape=jax.ShapeDtypeStruct((TOP_K, t), I32),
        compiler_params=_params("arbitrary"),
        name="positions",
    )(first_row, eid, rank)


def _sc_mesh():
    return plsc.VectorSubcoreMesh(core_axis_name="core", subcore_axis_name="subcore")


def _dispatch_rows(xa, xb, dest, n_rows):
    t, width = xa.shape
    n_picks = dest.shape[0]
    out = jax.ShapeDtypeStruct((n_rows, width), xa.dtype)

    @functools.partial(pl.kernel, out_type=[out, out], mesh=_sc_mesh(), scratch_types=[])
    def scatter(xa_hbm, xb_hbm, i_hbm, oa_hbm, ob_hbm):
        for x_hbm, o_hbm in ((xa_hbm, oa_hbm), (xb_hbm, ob_hbm)):
            def body(x_vmem, i_vmem, o_hbm=o_hbm):
                for k in range(n_picks):
                    pltpu.sync_copy(x_vmem, o_hbm.at[i_vmem.at[k]])

            pltpu.emit_pipeline(
                body,
                grid=(t // SC_WINDOW,),
                in_specs=[pl.BlockSpec((SC_WINDOW, width), lambda i: (i, 0)),
                          pl.BlockSpec((n_picks, SC_WINDOW), lambda i: (0, i))],
                out_specs=[],
                core_axis_name=("core", "subcore"),
                dimension_semantics=(pltpu.PARALLEL,),
            )(x_hbm, i_hbm)

    return scatter(xa, xb, dest)


def _gather_rows(ya, yb, idx):
    n = idx.shape[1]
    width = ya.shape[1]
    out = jax.ShapeDtypeStruct((n, width), ya.dtype)

    @functools.partial(pl.kernel, out_type=[out, out], mesh=_sc_mesh(), scratch_types=[])
    def gather(ya_hbm, yb_hbm, i_hbm, oa_hbm, ob_hbm):
        for y_hbm, o_hbm in ((ya_hbm, oa_hbm), (yb_hbm, ob_hbm)):
            def body(i_vmem, o_vmem, y_hbm=y_hbm):
                pltpu.sync_copy(y_hbm.at[i_vmem.at[0]], o_vmem)

            pltpu.emit_pipeline(
                body,
                grid=(n // SC_WINDOW,),
                in_specs=[pl.BlockSpec((1, SC_WINDOW), lambda i: (0, i))],
                out_specs=[pl.BlockSpec((SC_WINDOW, width), lambda i: (i, 0))],
                core_axis_name=("core", "subcore"),
                dimension_semantics=(pltpu.PARALLEL,),
            )(i_hbm, o_hbm)

    return gather(ya, yb, idx)


def _expert_kernel(first_tile_ref, n_tiles_ref, total_ref, tile_rows_ref, tile_start_ref, xa_hbm, xb_hbm, w1_ref,
                   w3_ref, w2_ref,
                   ya_hbm, yb_hbm, xa_buf, xb_buf, ya_buf, yb_buf, in_sem, out_sem, w1_bf, w3_bf, w2_bf):
    e = pl.program_id(0)
    tm = EXPERT_TILE
    ring = EXPERT_RING
    total = total_ref[0]
    first_tile = first_tile_ref[e]

    def pieces(g, slot, bufs_hbm, sems, to_vmem):
        out = []
        for r0 in range(0, tm, EXPERT_COPY_ROWS):
            hbm_rows = pl.ds(pl.multiple_of(tile_start_ref[g] + r0, EXPERT_COPY_ROWS), EXPERT_COPY_ROWS)
            for i, (buf, hbm) in enumerate(bufs_hbm):
                vmem = buf.at[slot, pl.ds(r0, EXPERT_COPY_ROWS)]
                src, dst = (hbm.at[hbm_rows], vmem) if to_vmem else (vmem, hbm.at[hbm_rows])
                out.append((r0, pltpu.make_async_copy(src, dst, sems.at[i, slot])))
        return out

    def for_live_pieces(g, bufs_hbm, sems, to_vmem, act):
        slot = g % ring
        live_rows = tile_rows_ref[g]
        for r0, copy in pieces(g, slot, bufs_hbm, sems, to_vmem):
            pl.when(r0 < live_rows)(functools.partial(act, copy))

    ins = ((xa_buf, xa_hbm), (xb_buf, xb_hbm))
    outs = ((ya_buf, ya_hbm), (yb_buf, yb_hbm))
    start_in = lambda g: for_live_pieces(g, ins, in_sem, True, lambda c: c.start())
    wait_in = lambda g: for_live_pieces(g, ins, in_sem, True, lambda c: c.wait())
    start_out = lambda g: for_live_pieces(g, outs, out_sem, False, lambda c: c.start())
    wait_out = lambda g: for_live_pieces(g, outs, out_sem, False, lambda c: c.wait())

    @pl.when(n_tiles_ref[e] > 0)
    def _():
        w1_bf[...] = w1_ref[0].astype(BF16)
        w3_bf[...] = w3_ref[0].astype(BF16)
        w2_bf[...] = w2_ref[0].astype(BF16)

    def tile(t, carry):
        g = first_tile + t
        slot = g % ring

        @pl.when(g == 0)
        def _():
            for ahead in range(ring - 1):
                pl.when(ahead < total)(functools.partial(start_in, ahead))

        pl.when(g + ring - 1 < total)(functools.partial(start_in, g + ring - 1))
        wait_in(g)
        pl.when(g >= ring)(functools.partial(wait_out, g - ring))

        def compute(m):
            rows = pl.ds(0, m)
            hi, lo = _unpack_bf16_pairs(jnp.concatenate([xa_buf[slot, rows], xb_buf[slot, rows]], axis=1))
            hi = hi.astype(BF16)
            lo = lo.astype(BF16)

            def up(w_bf):
                return (jnp.dot(hi, w_bf[:PACKED, :], preferred_element_type=F32)
                        + jnp.dot(lo, w_bf[PACKED:, :], preferred_element_type=F32))

            act = _silu(up(w1_bf)) * up(w3_bf)
            words = _pack_bf16_pairs(_mm(act, w2_bf[...]))
            ya_buf[slot, rows] = words[:, :HALF]
            yb_buf[slot, rows] = words[:, HALF:]

        live_rows = tile_rows_ref[g]
        for m in range(EXPERT_COPY_ROWS, tm + 1, EXPERT_COPY_ROWS):
            pl.when(jnp.logical_and(live_rows > m - EXPERT_COPY_ROWS, live_rows <= m))(
                functools.partial(compute, m))
        start_out(g)
        return carry

    lax.fori_loop(0, n_tiles_ref[e], tile, 0)

    @pl.when(e == pl.num_programs(0) - 1)
    def _():
        for back in range(ring, 0, -1):
            pl.when(total >= back)(functools.partial(wait_out, total - back))


def _experts(xa, xb, first_tile, n_tiles, total_tiles, tile_rows, tile_start, w1, w3, w2):
    rows = xa.shape[0]
    d, f = D_MODEL, D_EXPERT
    tm = EXPERT_TILE
    weights = lambda a, b: pl.BlockSpec((1, a, b), lambda e, *_: (e, 0, 0))
    ring = lambda: pltpu.VMEM((EXPERT_RING, tm, HALF), I32)
    sems = lambda: pltpu.SemaphoreType.DMA((2, EXPERT_RING))
    return pl.pallas_call(
        _expert_kernel,
        grid_spec=pltpu.PrefetchScalarGridSpec(
            num_scalar_prefetch=5,
            grid=(N_EXPERTS,),
            in_specs=[
                pl.BlockSpec(memory_space=pl.ANY), pl.BlockSpec(memory_space=pl.ANY),
                weights(d, f), weights(d, f), weights(f, d),
            ],
            out_specs=[pl.BlockSpec(memory_space=pl.ANY), pl.BlockSpec(memory_space=pl.ANY)],
            scratch_shapes=[
                ring(), ring(), ring(), ring(), sems(), sems(),
                pltpu.VMEM((d, f), BF16), pltpu.VMEM((d, f), BF16), pltpu.VMEM((f, d), BF16),
            ],
        ),
        out_shape=[jax.ShapeDtypeStruct((rows, HALF), I32), jax.ShapeDtypeStruct((rows, HALF), I32)],
        compiler_params=_params("arbitrary"),
        name="experts",
    )(first_tile, n_tiles, total_tiles, tile_rows, tile_start, xa, xb, w1, w3, w2)


def _final_kernel(ha_ref, hb_ref, x1_ref, ada_ref, gpost2_ref, w8_ref, yga_ref, ygb_ref, ws1_ref, ws3_ref,
                  ws2_ref, out_ref):
    hi, lo = _unpack_bf16_pairs(jnp.concatenate([ha_ref[...], hb_ref[...]], axis=1))
    hi = hi.astype(BF16)
    lo = lo.astype(BF16)

    def up(w_ref):
        return (jnp.dot(hi, w_ref[:PACKED, :], preferred_element_type=F32)
                + jnp.dot(lo, w_ref[PACKED:, :], preferred_element_type=F32))

    f = _mm(_silu(up(ws1_ref)) * up(ws3_ref), ws2_ref[...])
    for k in range(TOP_K):
        hi, lo = _unpack_bf16_pairs(jnp.concatenate([yga_ref[k], ygb_ref[k]], axis=1))
        f = f + w8_ref[:, k:k + 1] * jnp.concatenate([hi, lo], axis=1)
    out_ref[...] = x1_ref[...] + ada_ref[5, 0] * _rmsnorm(f, gpost2_ref[...])


def _final(ha, hb, x1, ada4, tokens_per_ada_block, w8_t, yga, ygb, first_token, p, tm):
    t, d = x1.shape
    f = D_EXPERT
    first_block = first_token // tm
    tok = lambda: pl.BlockSpec((tm, d), lambda i: (i, 0))
    if ada4.shape[2] == 1:
        ada_spec = pl.BlockSpec((6, 1, 1, d), lambda i: (0, i * tm // tokens_per_ada_block, 0, 0))
    else:
        ada_spec = pl.BlockSpec((6, 1, tm, d), lambda i: (0, 0, i, 0))
    gathered = lambda: pl.BlockSpec((TOP_K, tm, HALF), lambda i: (0, first_block + i, 0))
    return pl.pallas_call(
        _final_kernel,
        grid=(t // tm,),
        in_specs=[
            pl.BlockSpec((tm, HALF), lambda i: (first_block + i, 0)),
            pl.BlockSpec((tm, HALF), lambda i: (first_block + i, 0)),
            tok(), ada_spec, _resident((1, d)),
            pl.BlockSpec((tm, TOP_K), lambda i: (first_block + i, 0)),
            gathered(), gathered(),
            _resident((d, f)), _resident((d, f)), _resident((f, d)),
        ],
        out_specs=tok(),
        out_shape=jax.ShapeDtypeStruct((t, d), F32),
        compiler_params=_params("arbitrary"),
        name="final",
    )(ha, hb, x1, ada4, p["g_post2"], w8_t, yga, ygb, p["ws1"], p["ws3"], p["ws2"])


def kernel(x_prompt, x_sample, state_sconv, state_cconv, c_prompt, c_sample, w_ada, b_ada, g_pre1, g_post1,
           g_pre2, g_post2, w_in, w_sconv, w_out_a, w_cconv, b_cconv, ln_g, ln_b, w_out_b, b_out_b, w_o,
           w_router, e_bias, w1, w3, w2, ws1, ws3, ws2):
    assert w_ada.shape[0] == 1, "single-layer trunk"
    nb, seq, d = x_prompt.shape
    ns = x_sample.shape[0]
    n_prompt = nb * seq
    n_tokens = n_prompt + ns
    p = {
        "g_pre1": g_pre1, "g_post1": g_post1, "g_pre2": g_pre2, "g_post2": g_post2,
        "w_in": w_in[0].astype(BF16), "w_sconv": w_sconv[0], "w_out_a": w_out_a[0].astype(BF16),
        "w_cconv": w_cconv[0], "b_cconv": b_cconv, "ln_g": ln_g, "ln_b": ln_b,
        "w_out_b": w_out_b[0].astype(BF16), "b_out_b": b_out_b, "w_o": w_o[0].astype(BF16),
        "ws1": ws1[0].astype(BF16), "ws3": ws3[0].astype(BF16), "ws2": ws2[0].astype(BF16),
    }
    w_router_t = w_router[0].T.astype(BF16)
    e_bias_col = e_bias[0].reshape(N_EXPERTS, 1)

    ada = _ada(jnp.concatenate([c_prompt, c_sample], axis=0), w_ada[0], b_ada)
    ada = ada.reshape(nb + ns, 6, d).transpose(1, 0, 2)
    ada_p = ada[:, :nb].reshape(6, nb, 1, d)
    ada_s = ada[:, nb:].reshape(6, 1, ns, d)

    x1_p, ha, hb, u_tail, glu_tail = _prompt_mixer(x_prompt, ada_p, n_tokens, p)
    x1_s, ha, hb, new_sconv_sample, new_cconv_sample = _sample_mixer(
        x_sample.reshape(ns, d), ada_s, state_sconv[0], state_cconv[0], ha, hb, n_prompt, p)

    eid, rank, w8_t, cnt = _router(ha, hb, w_router_t, e_bias_col, ROUTER_TILE)
    counts = cnt[:, 0].astype(I32)
    n_tiles = (counts + EXPERT_TILE - 1) // EXPERT_TILE
    tile_ends = jnp.cumsum(n_tiles)
    first_tile = tile_ends - n_tiles
    n_rows = n_tokens * TOP_K + N_EXPERTS * EXPERT_COPY_ROWS
    tile_ids = jnp.arange(pl.cdiv(n_tokens * TOP_K, EXPERT_TILE) + N_EXPERTS, dtype=I32)
    tile_owner = jnp.minimum(jnp.sum(tile_ids[:, None] >= tile_ends[None, :], axis=1), N_EXPERTS - 1)
    tile_rows = jnp.clip(counts[tile_owner] - (tile_ids - first_tile[tile_owner]) * EXPERT_TILE, 0, EXPERT_TILE)
    tile_rows = jnp.where(tile_ids < tile_ends[-1], tile_rows, 0).astype(I32)
    tile_span = (tile_rows + EXPERT_COPY_ROWS - 1) // EXPERT_COPY_ROWS * EXPERT_COPY_ROWS
    tile_start = (jnp.cumsum(tile_span) - tile_span).astype(I32)
    dest = _positions(eid, rank, tile_start[first_tile])

    xa, xb = _dispatch_rows(ha, hb, dest, n_rows)
    ya, yb = _experts(xa, xb, first_tile, n_tiles, tile_ends[-1:], tile_rows, tile_start, w1[0], w3[0], w2[0])
    flat = dest.reshape(1, TOP_K * n_tokens)
    yga, ygb = (g.reshape(TOP_K, n_tokens, HALF) for g in _gather_rows(ya, yb, flat))

    y_p = _final(ha, hb, x1_p.reshape(n_prompt, d), ada_p, seq, w8_t, yga, ygb, 0, p, FINAL_TILE)
    y_s = _final(ha, hb, x1_s, ada_s, 1, w8_t, yga, ygb, n_prompt, p, ns)

    new_sconv_prompt = u_tail[:, SUBLANES - (K_SCONV - 1):][None]
    new_cconv_prompt = glu_tail[:, CONV_HALO - (K_CCONV - 1):][None]
    return (y_p.reshape(nb, seq, d), y_s.reshape(ns, 1, d),
            new_sconv_prompt, new_sconv_sample[None], new_cconv_prompt, new_cconv_sample[None])
```

```python
import functools

import jax
import jax.numpy as jnp
from jax import lax
from jax.experimental import pallas as pl
from jax.experimental.pallas import tpu as pltpu
from jax.experimental.pallas import tpu_sc as plsc

D_MODEL = 1024
K_SCONV = 3
K_CCONV = 31
N_EXPERTS = 64
TOP_K = 8
N_GROUPS = 8
GROUP_SIZE = N_EXPERTS // N_GROUPS
TOPK_GROUPS = 4
D_EXPERT = 256
ROUTED_SCALE = 2.5
RMS_EPS = 1e-6
LN_EPS = 1e-5

F32 = jnp.float32
BF16 = jnp.bfloat16
I32 = jnp.int32
NEG_INF = float("-inf")

VMEM_LIMIT_BYTES_V7X = 56 * 1024 * 1024
SUBLANES = 8
LANES = 128

MIXER_TILE = 512
CONV_HALO = 32
CONV_ROW_CHUNK = 64
CONV_LANE_CHUNK = 128
PROJ_LANE_CHUNK = 256
ROUTER_TILE = 384
POSITION_TILE = 5504
EXPERT_TILE = 768
EXPERT_COPY_ROWS = 256
EXPERT_RING = 4
FINAL_TILE = 512
PACKED = D_MODEL // 2
HALF = PACKED // 2
SC_WINDOW = 128


def _params(*semantics):
    return pltpu.CompilerParams(dimension_semantics=semantics, vmem_limit_bytes=VMEM_LIMIT_BYTES_V7X)


def _resident(shape):
    zeros = (0,) * len(shape)
    return pl.BlockSpec(shape, lambda *_: zeros, pipeline_mode=pl.Buffered(1))


def _sigmoid(x):
    return 0.5 * jnp.tanh(0.5 * x) + 0.5


def _silu(x):
    return x * _sigmoid(x)


def _rmsnorm(x, g):
    return x * lax.rsqrt(jnp.mean(x * x, axis=-1, keepdims=True) + RMS_EPS) * g


def _layernorm(x, g, b):
    mu = jnp.mean(x, axis=-1, keepdims=True)
    xc = x - mu
    var = jnp.mean(xc * xc, axis=-1, keepdims=True)
    return xc * lax.rsqrt(var + LN_EPS) * g + b


def _mm(a, w):
    return jnp.dot(a.astype(BF16), w.astype(BF16), preferred_element_type=F32)


def _pack_bf16_pairs(v):
    n = v.shape[1] // 2
    bits = lax.bitcast_convert_type(v.astype(BF16).astype(F32), I32)
    return bits[:, :n] | lax.shift_right_logical(bits[:, n:], 16)


def _unpack_bf16_pairs(words):
    hi = lax.bitcast_convert_type(words & jnp.int32(-65536), F32)
    lo = lax.bitcast_convert_type(lax.shift_left(words, 16), F32)
    return hi, lo


def _ada_kernel(c_ref, w_ref, b_ref, o_ref):
    o_ref[...] = _mm(_silu(c_ref[...]), w_ref[...]) + b_ref[...]


def _ada(c, w_ada, b_ada):
    n = c.shape[0]
    width = w_ada.shape[1]
    bn = D_MODEL
    return pl.pallas_call(
        _ada_kernel,
        grid=(width // bn,),
        in_specs=[
            pl.BlockSpec((n, D_MODEL), lambda i: (0, 0)),
            pl.BlockSpec((D_MODEL, bn), lambda i: (0, i)),
            pl.BlockSpec((1, bn), lambda i: (0, i)),
        ],
        out_specs=pl.BlockSpec((n, bn), lambda i: (0, i)),
        out_shape=jax.ShapeDtypeStruct((n, width), F32),
        compiler_params=_params("arbitrary"),
        name="ada",
    )(c, w_ada, b_ada)


def _mixer_tail(x, ada_ref, y_a, y_b, gate_a, gate_b, w_o_ref, gpost1_ref, gpre2_ref, x1_ref, ha_ref, hb_ref):
    m = gate_a * y_a + gate_b * y_b
    t = _mm(m, w_o_ref[...])
    x1 = x + ada_ref[2, 0] * _rmsnorm(t, gpost1_ref[...])
    h2 = _rmsnorm(x1, gpre2_ref[...]) * (1.0 + ada_ref[4, 0]) + ada_ref[3, 0]
    x1_ref[...] = x1.reshape(x1_ref.shape)
    words = _pack_bf16_pairs(h2)
    ha_ref[...] = words[:, :HALF].reshape(ha_ref.shape)
    hb_ref[...] = words[:, HALF:].reshape(hb_ref.shape)


def _prompt_mixer_kernel(x_ref, ada_ref, gpre1_ref, gpost1_ref, gpre2_ref, w_in_ref, w_sconv_ref,
                         w_out_a_ref, w_cconv_ref, b_cconv_ref, ln_g_ref, ln_b_ref, w_out_b_ref,
                         b_out_b_ref, w_o_ref, ha_init_ref, hb_init_ref,
                         x1_ref, ha_ref, hb_ref, u_tail_ref, glu_tail_ref,
                         u_ext, glu_ext, cv_buf, h_buf, av_buf, ga_buf, gb_buf):
    del ha_init_ref, hb_init_ref
    tm = x_ref.shape[1]
    d = D_MODEL

    @pl.when(pl.program_id(1) == 0)
    def _():
        u_ext[pl.ds(0, SUBLANES), :] = jnp.zeros((SUBLANES, d), F32)
        glu_ext[pl.ds(0, CONV_HALO), :] = jnp.zeros((CONV_HALO, d), F32)

    h_buf[...] = (_rmsnorm(x_ref[0], gpre1_ref[...]) * (1.0 + ada_ref[1, 0]) + ada_ref[0, 0]).astype(BF16)

    def proj(i, lanes=slice(0, d)):
        return jnp.dot(h_buf[...], w_in_ref[:, i * d + lanes.start:i * d + lanes.stop],
                       preferred_element_type=F32)

    glu = proj(3) * _sigmoid(proj(4))
    glu_ext[pl.ds(CONV_HALO, tm), :] = glu
    glu_tail_ref[0] = glu[tm - CONV_HALO:, :]
    first = CONV_HALO - (K_CCONV - 1)

    def conv_piece(r0, lanes):
        acc = None
        for res in range(SUBLANES):
            rows = CONV_ROW_CHUNK + (SUBLANES if res else 0)
            part = None
            for k in range(K_CCONV):
                if (first + k) % SUBLANES != res:
                    continue
                start = r0 + first + k - res
                term = w_cconv_ref[k:k + 1, lanes] * glu_ext[start:start + rows, lanes]
                part = term if part is None else part + term
            part = part[res:res + CONV_ROW_CHUNK, :]
            acc = part if acc is None else acc + part
        cv_buf[r0:r0 + CONV_ROW_CHUNK, lanes] = acc

    def branch_a(lanes):
        u = proj(2, lanes) * proj(0, lanes)
        u_ext[pl.ds(SUBLANES, tm), lanes] = u
        v = (w_sconv_ref[0:1, lanes] * u_ext[pl.ds(SUBLANES - 2, tm), lanes]
             + w_sconv_ref[1:2, lanes] * u_ext[pl.ds(SUBLANES - 1, tm), lanes]
             + w_sconv_ref[2:3, lanes] * u)
        u_tail_ref[0, :, lanes] = u[tm - SUBLANES:, :]
        u_ext[pl.ds(0, SUBLANES), lanes] = u[tm - SUBLANES:, :]
        av_buf[:, lanes] = (proj(1, lanes) * v).astype(BF16)

    def gates(lanes):
        ga_buf[:, lanes] = _sigmoid(proj(5, lanes))
        gb_buf[:, lanes] = _sigmoid(proj(6, lanes))

    for c in range(0, d, PROJ_LANE_CHUNK):
        lanes = slice(c, c + PROJ_LANE_CHUNK)

        @pl.when(pl.program_id(1) + c >= 0)
        def _(lanes=lanes):
            for r0 in range(0, tm, CONV_ROW_CHUNK):
                for l0 in range(lanes.start, lanes.stop, CONV_LANE_CHUNK):
                    conv_piece(r0, slice(l0, l0 + CONV_LANE_CHUNK))
            branch_a(lanes)
            gates(lanes)

    glu_ext[pl.ds(0, CONV_HALO), :] = glu_ext[pl.ds(tm, CONV_HALO), :]

    y_a = jnp.dot(av_buf[...], w_out_a_ref[...], preferred_element_type=F32)
    cv = cv_buf[...] + b_cconv_ref[...]
    y_b = _mm(_silu(_layernorm(cv, ln_g_ref[...], ln_b_ref[...])), w_out_b_ref[...]) + b_out_b_ref[...]

    _mixer_tail(x_ref[0], ada_ref, y_a, y_b, ga_buf[...], gb_buf[...], w_o_ref, gpost1_ref, gpre2_ref,
                x1_ref, ha_ref, hb_ref)


def _prompt_mixer(x, ada4, n_tokens, p):
    nb, seq, d = x.shape
    tm = MIXER_TILE
    row = lambda: _resident((1, d))
    tok = lambda w: pl.BlockSpec((1, tm, w), lambda b, j: (b, j, 0))
    packed = lambda: pl.BlockSpec((tm, HALF), lambda b, j: (b * (seq // tm) + j, 0))
    return pl.pallas_call(
        _prompt_mixer_kernel,
        grid=(nb, seq // tm),
        in_specs=[
            tok(d),
            pl.BlockSpec((6, 1, 1, d), lambda b, j: (0, b, 0, 0)),
            row(), row(), row(),
            _resident(p["w_in"].shape),
            _resident((K_SCONV, d)),
            _resident((d, d)),
            _resident((K_CCONV, d)),
            row(), row(), row(),
            _resident((d, d)),
            row(),
            _resident((d, d)),
            pl.BlockSpec(memory_space=pl.ANY), pl.BlockSpec(memory_space=pl.ANY),
        ],
        out_specs=[
            tok(d), packed(), packed(),
            pl.BlockSpec((1, SUBLANES, d), lambda b, j: (b, 0, 0)),
            pl.BlockSpec((1, CONV_HALO, d), lambda b, j: (b, 0, 0)),
        ],
        out_shape=[
            jax.ShapeDtypeStruct((nb, seq, d), F32),
            jax.ShapeDtypeStruct((n_tokens, HALF), I32),
            jax.ShapeDtypeStruct((n_tokens, HALF), I32),
            jax.ShapeDtypeStruct((nb, SUBLANES, d), F32),
            jax.ShapeDtypeStruct((nb, CONV_HALO, d), F32),
        ],
        scratch_shapes=[
            pltpu.VMEM((tm + SUBLANES, d), F32),
            pltpu.VMEM((tm + CONV_HALO + SUBLANES, d), F32),
            pltpu.VMEM((tm, d), F32),
            pltpu.VMEM((tm, d), BF16),
            pltpu.VMEM((tm, d), BF16),
            pltpu.VMEM((tm, d), F32),
            pltpu.VMEM((tm, d), F32),
        ],
        input_output_aliases={15: 1, 16: 2},
        compiler_params=_params("arbitrary", "arbitrary"),
        name="prompt_mixer",
    )(x, ada4, p["g_pre1"], p["g_post1"], p["g_pre2"], p["w_in"], p["w_sconv"], p["w_out_a"],
      p["w_cconv"], p["b_cconv"], p["ln_g"], p["ln_b"], p["w_out_b"], p["b_out_b"], p["w_o"],
      jnp.zeros((n_tokens, HALF), I32), jnp.zeros((n_tokens, HALF), I32))


def _sample_mixer_kernel(x_ref, ada_ref, st_a_ref, st_b_ref, gpre1_ref, gpost1_ref, gpre2_ref, w_in_ref,
                         w_sconv_ref, w_out_a_ref, w_cconv_ref, b_cconv_ref, ln_g_ref, ln_b_ref,
                         w_out_b_ref, b_out_b_ref, w_o_ref, ha_all_ref, hb_all_ref,
                         x1_ref, ha_ref, hb_ref, new_a_ref, new_b_ref):
    del ha_all_ref, hb_all_ref
    d = D_MODEL
    x = x_ref[...]
    h = (_rmsnorm(x, gpre1_ref[...]) * (1.0 + ada_ref[1, 0]) + ada_ref[0, 0]).astype(BF16)

    def proj(i):
        return jnp.dot(h, w_in_ref[:, i * d:(i + 1) * d], preferred_element_type=F32)

    u = proj(2) * proj(0)
    v = w_sconv_ref[K_SCONV - 1:K_SCONV, :] * u
    for k in range(K_SCONV - 1):
        v = v + w_sconv_ref[k:k + 1, :] * st_a_ref[:, k, :]
    new_a_ref[:, :K_SCONV - 2, :] = st_a_ref[:, 1:, :]
    new_a_ref[:, K_SCONV - 2, :] = u
    y_a = _mm(proj(1) * v, w_out_a_ref[...])

    glu = proj(3) * _sigmoid(proj(4))
    cv = w_cconv_ref[K_CCONV - 1:K_CCONV, :] * glu + b_cconv_ref[...]
    for k in range(K_CCONV - 1):
        cv = cv + w_cconv_ref[k:k + 1, :] * st_b_ref[:, k, :]
    new_b_ref[:, :K_CCONV - 2, :] = st_b_ref[:, 1:, :]
    new_b_ref[:, K_CCONV - 2, :] = glu
    y_b = _mm(_silu(_layernorm(cv, ln_g_ref[...], ln_b_ref[...])), w_out_b_ref[...]) + b_out_b_ref[...]

    _mixer_tail(x, ada_ref, y_a, y_b, _sigmoid(proj(5)), _sigmoid(proj(6)), w_o_ref, gpost1_ref,
                gpre2_ref, x1_ref, ha_ref, hb_ref)


def _sample_mixer(x, ada4, st_a, st_b, ha_all, hb_all, first_token, p):
    n, d = x.shape
    tb = 32
    first_block = first_token // tb
    row = lambda: _resident((1, d))
    tok = lambda w: pl.BlockSpec((tb, w), lambda i: (i, 0))
    state = lambda k: pl.BlockSpec((tb, k - 1, d), lambda i: (i, 0, 0))
    packed = lambda: pl.BlockSpec((tb, HALF), lambda i: (first_block + i, 0))
    return pl.pallas_call(
        _sample_mixer_kernel,
        grid=(n // tb,),
        in_specs=[
            tok(d),
            pl.BlockSpec((6, 1, tb, d), lambda i: (0, 0, i, 0)),
            state(K_SCONV), state(K_CCONV),
            row(), row(), row(),
            _resident(p["w_in"].shape),
            _resident((K_SCONV, d)),
            _resident((d, d)),
            _resident((K_CCONV, d)),
            row(), row(), row(),
            _resident((d, d)),
            row(),
            _resident((d, d)),
            pl.BlockSpec(memory_space=pl.ANY), pl.BlockSpec(memory_space=pl.ANY),
        ],
        out_specs=[tok(d), packed(), packed(), state(K_SCONV), state(K_CCONV)],
        out_shape=[
            jax.ShapeDtypeStruct((n, d), F32),
            jax.ShapeDtypeStruct(ha_all.shape, I32),
            jax.ShapeDtypeStruct(hb_all.shape, I32),
            jax.ShapeDtypeStruct(st_a.shape, F32),
            jax.ShapeDtypeStruct(st_b.shape, F32),
        ],
        input_output_aliases={17: 1, 18: 2},
        compiler_params=_params("arbitrary"),
        name="sample_mixer",
    )(x, ada4, st_a, st_b, p["g_pre1"], p["g_post1"], p["g_pre2"], p["w_in"], p["w_sconv"], p["w_out_a"],
      p["w_cconv"], p["b_cconv"], p["ln_g"], p["ln_b"], p["w_out_b"], p["b_out_b"], p["w_o"], ha_all, hb_all)


def _first_index_of_max(work, iota, axis, limit):
    m = jnp.max(work, axis=axis, keepdims=True)
    return m, jnp.min(jnp.where(work == m, iota, limit), axis=axis, keepdims=True)


def _router_kernel(ha_ref, hb_ref, wr_ref, bias_ref, eid_ref, rank_ref, w8_ref, cnt_ref):
    tm = ha_ref.shape[0]
    hi, lo = _unpack_bf16_pairs(jnp.concatenate([ha_ref[...], hb_ref[...]], axis=1))
    nt = (((1,), (1,)), ((), ()))
    logits = (lax.dot_general(wr_ref[:, :PACKED], hi.astype(BF16), nt, preferred_element_type=F32)
              + lax.dot_general(wr_ref[:, PACKED:], lo.astype(BF16), nt, preferred_element_type=F32))
    scores = _sigmoid(logits)
    sel = scores + bias_ref[...]

    sel3 = sel.reshape(N_GROUPS, GROUP_SIZE, tm)
    io3 = lax.broadcasted_iota(I32, sel3.shape, 1)
    m1, i1 = _first_index_of_max(sel3, io3, 1, GROUP_SIZE)
    m2 = jnp.max(jnp.where(io3 == i1, NEG_INF, sel3), axis=1, keepdims=True)
    gscore = (m1 + m2).reshape(N_GROUPS, tm)

    iog = lax.broadcasted_iota(I32, gscore.shape, 0)
    gmask = jnp.zeros(gscore.shape, jnp.bool_)
    for _ in range(TOPK_GROUPS):
        _, gi = _first_index_of_max(gscore, iog, 0, N_GROUPS)
        pick = iog == gi
        gmask = jnp.logical_or(gmask, pick)
        gscore = jnp.where(pick, NEG_INF, gscore)

    work = jnp.where(gmask.reshape(N_GROUPS, 1, tm), sel3, NEG_INF).reshape(N_EXPERTS, tm)
    ioe = lax.broadcasted_iota(I32, work.shape, 0)
    chosen = jnp.zeros(work.shape, jnp.bool_)
    eids, ws, picks = [], [], []
    for _ in range(TOP_K):
        _, ei = _first_index_of_max(work, ioe, 0, N_EXPERTS)
        pick = ioe == ei
        eids.append(ei)
        picks.append(pick)
        ws.append(jnp.sum(jnp.where(pick, scores, 0.0), axis=0, keepdims=True))
        chosen = jnp.logical_or(chosen, pick)
        work = jnp.where(pick, NEG_INF, work)

    w = jnp.concatenate(ws, axis=0)
    eid_ref[...] = jnp.concatenate(eids, axis=0)
    w8_ref[...] = (w / jnp.sum(w, axis=0, keepdims=True) * ROUTED_SCALE).T

    @pl.when(pl.program_id(0) == 0)
    def _():
        cnt_ref[...] = jnp.zeros(cnt_ref.shape, F32)

    before = lax.broadcasted_iota(I32, (tm, tm), 0) < lax.broadcasted_iota(I32, (tm, tm), 1)
    seen = cnt_ref[:, 0:1] + jnp.dot(chosen.astype(BF16), before.astype(BF16), preferred_element_type=F32)
    rank_ref[...] = jnp.concatenate(
        [jnp.sum(jnp.where(pick, seen, 0.0), axis=0, keepdims=True) for pick in picks], axis=0)
    cnt_ref[...] += jnp.sum(chosen.astype(F32), axis=1, keepdims=True)


def _router(ha, hb, w_router_t, e_bias_col, tm):
    t = ha.shape[0]
    d = D_MODEL
    picks = lambda: pl.BlockSpec((TOP_K, tm), lambda i: (0, i))
    half_rows = lambda: pl.BlockSpec((tm, HALF), lambda i: (i, 0))
    return pl.pallas_call(
        _router_kernel,
        grid=(t // tm,),
        in_specs=[
            half_rows(), half_rows(),
            _resident((N_EXPERTS, d)),
            _resident((N_EXPERTS, 1)),
        ],
        out_specs=[picks(), picks(), pl.BlockSpec((tm, TOP_K), lambda i: (i, 0)),
                   pl.BlockSpec((N_EXPERTS, LANES), lambda i: (0, 0))],
        out_shape=[
            jax.ShapeDtypeStruct((TOP_K, t), I32),
            jax.ShapeDtypeStruct((TOP_K, t), F32),
            jax.ShapeDtypeStruct((t, TOP_K), F32),
            jax.ShapeDtypeStruct((N_EXPERTS, LANES), F32),
        ],
        compiler_params=_params("arbitrary"),
        name="router",
    )(ha, hb, w_router_t, e_bias_col)


def _offset_kernel(off_ref, eid_ref, rank_ref, dest_ref):
    eid = eid_ref[...]
    dest = rank_ref[...].astype(I32)
    for e in range(N_EXPERTS):
        dest = dest + jnp.where(eid == e, off_ref[e], 0)
    dest_ref[...] = dest


def _positions(eid, rank, first_row):
    t = eid.shape[1]
    tm = POSITION_TILE
    picks = lambda: pl.BlockSpec((TOP_K, tm), lambda i, off: (0, i))
    return pl.pallas_call(
        _offset_kernel,
        grid_spec=pltpu.PrefetchScalarGridSpec(
            num_scalar_prefetch=1, grid=(t // tm,), in_specs=[picks(), picks()], out_specs=picks()),
        out_shape=jax.ShapeDtypeStruct((TOP_K, t), I32),
        compiler_params=_params("arbitrary"),
        name="positions",
    )(first_row, eid, rank)


def _sc_mesh():
    return plsc.VectorSubcoreMesh(core_axis_name="core", subcore_axis_name="subcore")


def _dispatch_rows(xa, xb, dest, n_rows):
    t, width = xa.shape
    n_picks = dest.shape[0]
    out = jax.ShapeDtypeStruct((n_rows, width), xa.dtype)

    @functools.partial(pl.kernel, out_type=[out, out], mesh=_sc_mesh(), scratch_types=[])
    def scatter(xa_hbm, xb_hbm, i_hbm, oa_hbm, ob_hbm):
        for x_hbm, o_hbm in ((xa_hbm, oa_hbm), (xb_hbm, ob_hbm)):
            def body(x_vmem, i_vmem, o_hbm=o_hbm):
                for k in range(n_picks):
                    pltpu.sync_copy(x_vmem, o_hbm.at[i_vmem.at[k]])

            pltpu.emit_pipeline(
                body,
                grid=(t // SC_WINDOW,),
                in_specs=[pl.BlockSpec((SC_WINDOW, width), lambda i: (i, 0)),
                          pl.BlockSpec((n_picks, SC_WINDOW), lambda i: (0, i))],
                out_specs=[],
                core_axis_name=("core", "subcore"),
                dimension_semantics=(pltpu.PARALLEL,),
            )(x_hbm, i_hbm)

    return scatter(xa, xb, dest)


def _gather_rows(ya, yb, idx):
    n = idx.shape[1]
    width = ya.shape[1]
    out = jax.ShapeDtypeStruct((n, width), ya.dtype)

    @functools.partial(pl.kernel, out_type=[out, out], mesh=_sc_mesh(), scratch_types=[])
    def gather(ya_hbm, yb_hbm, i_hbm, oa_hbm, ob_hbm):
        for y_hbm, o_hbm in ((ya_hbm, oa_hbm), (yb_hbm, ob_hbm)):
            def body(i_vmem, o_vmem, y_hbm=y_hbm):
                pltpu.sync_copy(y_hbm.at[i_vmem.at[0]], o_vmem)

            pltpu.emit_pipeline(
                body,
                grid=(n // SC_WINDOW,),
                in_specs=[pl.BlockSpec((1, SC_WINDOW), lambda i: (0, i))],
                out_specs=[pl.BlockSpec((SC_WINDOW, width), lambda i: (i, 0))],
                core_axis_name=("core", "subcore"),
                dimension_semantics=(pltpu.PARALLEL,),
            )(i_hbm, o_hbm)

    return gather(ya, yb, idx)


def _expert_kernel(first_tile_ref, n_tiles_ref, total_ref, tile_rows_ref, tile_start_ref, xa_hbm, xb_hbm, w1_ref,
                   w3_ref, w2_ref,
                   ya_hbm, yb_hbm, xa_buf, xb_buf, ya_buf, yb_buf, in_sem, out_sem, w1_bf, w3_bf, w2_bf):
    e = pl.program_id(0)
    tm = EXPERT_TILE
    ring = EXPERT_RING
    total = total_ref[0]
    first_tile = first_tile_ref[e]

    def pieces(g, slot, bufs_hbm, sems, to_vmem):
        out = []
        for r0 in range(0, tm, EXPERT_COPY_ROWS):
            hbm_rows = pl.ds(pl.multiple_of(tile_start_ref[g] + r0, EXPERT_COPY_ROWS), EXPERT_COPY_ROWS)
            for i, (buf, hbm) in enumerate(bufs_hbm):
                vmem = buf.at[slot, pl.ds(r0, EXPERT_COPY_ROWS)]
                src, dst = (hbm.at[hbm_rows], vmem) if to_vmem else (vmem, hbm.at[hbm_rows])
                out.append((r0, pltpu.make_async_copy(src, dst, sems.at[i, slot])))
        return out

    def for_live_pieces(g, bufs_hbm, sems, to_vmem, act):
        slot = g % ring
        live_rows = tile_rows_ref[g]
        for r0, copy in pieces(g, slot, bufs_hbm, sems, to_vmem):
            pl.when(r0 < live_rows)(functools.partial(act, copy))

    ins = ((xa_buf, xa_hbm), (xb_buf, xb_hbm))
    outs = ((ya_buf, ya_hbm), (yb_buf, yb_hbm))
    start_in = lambda g: for_live_pieces(g, ins, in_sem, True, lambda c: c.start())
    wait_in = lambda g: for_live_pieces(g, ins, in_sem, True, lambda c: c.wait())
    start_out = lambda g: for_live_pieces(g, outs, out_sem, False, lambda c: c.start())
    wait_out = lambda g: for_live_pieces(g, outs, out_sem, False, lambda c: c.wait())

    @pl.when(n_tiles_ref[e] > 0)
    def _():
        w1_bf[...] = w1_ref[0].astype(BF16)
        w3_bf[...] = w3_ref[0].astype(BF16)
        w2_bf[...] = w2_ref[0].astype(BF16)

    def tile(t, carry):
        g = first_tile + t
        slot = g % ring

        @pl.when(g == 0)
        def _():
            for ahead in range(ring - 1):
                pl.when(ahead < total)(functools.partial(start_in, ahead))

        pl.when(g + ring - 1 < total)(functools.partial(start_in, g + ring - 1))
        wait_in(g)
        pl.when(g >= ring)(functools.partial(wait_out, g - ring))

        def compute(m):
            rows = pl.ds(0, m)
            hi, lo = _unpack_bf16_pairs(jnp.concatenate([xa_buf[slot, rows], xb_buf[slot, rows]], axis=1))
            hi = hi.astype(BF16)
            lo = lo.astype(BF16)

            def up(w_bf):
                return (jnp.dot(hi, w_bf[:PACKED, :], preferred_element_type=F32)
                        + jnp.dot(lo, w_bf[PACKED:, :], preferred_element_type=F32))

            act = _silu(up(w1_bf)) * up(w3_bf)
            words = _pack_bf16_pairs(_mm(act, w2_bf[...]))
            ya_buf[slot, rows] = words[:, :HALF]
            yb_buf[slot, rows] = words[:, HALF:]

        live_rows = tile_rows_ref[g]
        for m in range(EXPERT_COPY_ROWS, tm + 1, EXPERT_COPY_ROWS):
            pl.when(jnp.logical_and(live_rows > m - EXPERT_COPY_ROWS, live_rows <= m))(
                functools.partial(compute, m))
        start_out(g)
        return carry

    lax.fori_loop(0, n_tiles_ref[e], tile, 0)

    @pl.when(e == pl.num_programs(0) - 1)
    def _():
        for back in range(ring, 0, -1):
            pl.when(total >= back)(functools.partial(wait_out, total - back))


def _experts(xa, xb, first_tile, n_tiles, total_tiles, tile_rows, tile_start, w1, w3, w2):
    rows = xa.shape[0]
    d, f = D_MODEL, D_EXPERT
    tm = EXPERT_TILE
    weights = lambda a, b: pl.BlockSpec((1, a, b), lambda e, *_: (e, 0, 0))
    ring = lambda: pltpu.VMEM((EXPERT_RING, tm, HALF), I32)
    sems = lambda: pltpu.SemaphoreType.DMA((2, EXPERT_RING))
    return pl.pallas_call(
        _expert_kernel,
        grid_spec=pltpu.PrefetchScalarGridSpec(
            num_scalar_prefetch=5,
            grid=(N_EXPERTS,),
            in_specs=[
                pl.BlockSpec(memory_space=pl.ANY), pl.BlockSpec(memory_space=pl.ANY),
                weights(d, f), weights(d, f), weights(f, d),
            ],
            out_specs=[pl.BlockSpec(memory_space=pl.ANY), pl.BlockSpec(memory_space=pl.ANY)],
            scratch_shapes=[
                ring(), ring(), ring(), ring(), sems(), sems(),
                pltpu.VMEM((d, f), BF16), pltpu.VMEM((d, f), BF16), pltpu.VMEM((f, d), BF16),
            ],
        ),
        out_shape=[jax.ShapeDtypeStruct((rows, HALF), I32), jax.ShapeDtypeStruct((rows, HALF), I32)],
        compiler_params=_params("arbitrary"),
        name="experts",
    )(first_tile, n_tiles, total_tiles, tile_rows, tile_start, xa, xb, w1, w3, w2)


def _final_kernel(ha_ref, hb_ref, x1_ref, ada_ref, gpost2_ref, w8_ref, yga_ref, ygb_ref, ws1_ref, ws3_ref,
                  ws2_ref, out_ref):
    hi, lo = _unpack_bf16_pairs(jnp.concatenate([ha_ref[...], hb_ref[...]], axis=1))
    hi = hi.astype(BF16)
    lo = lo.astype(BF16)

    def up(w_ref):
        return (jnp.dot(hi, w_ref[:PACKED, :], preferred_element_type=F32)
                + jnp.dot(lo, w_ref[PACKED:, :], preferred_element_type=F32))

    f = _mm(_silu(up(ws1_ref)) * up(ws3_ref), ws2_ref[...])
    for k in range(TOP_K):
        hi, lo = _unpack_bf16_pairs(jnp.concatenate([yga_ref[k], ygb_ref[k]], axis=1))
        f = f + w8_ref[:, k:k + 1] * jnp.concatenate([hi, lo], axis=1)
    out_ref[...] = x1_ref[...] + ada_ref[5, 0] * _rmsnorm(f, gpost2_ref[...])


def _final(ha, hb, x1, ada4, tokens_per_ada_block, w8_t, yga, ygb, first_token, p, tm):
    t, d = x1.shape
    f = D_EXPERT
    first_block = first_token // tm
    tok = lambda: pl.BlockSpec((tm, d), lambda i: (i, 0))
    if ada4.shape[2] == 1:
        ada_spec = pl.BlockSpec((6, 1, 1, d), lambda i: (0, i * tm // tokens_per_ada_block, 0, 0))
    else:
        ada_spec = pl.BlockSpec((6, 1, tm, d), lambda i: (0, 0, i, 0))
    gathered = lambda: pl.BlockSpec((TOP_K, tm, HALF), lambda i: (0, first_block + i, 0))
    return pl.pallas_call(
        _final_kernel,
        grid=(t // tm,),
        in_specs=[
            pl.BlockSpec((tm, HALF), lambda i: (first_block + i, 0)),
            pl.BlockSpec((tm, HALF), lambda i: (first_block + i, 0)),
            tok(), ada_spec, _resident((1, d)),
            pl.BlockSpec((tm, TOP_K), lambda i: (first_block + i, 0)),
            gathered(), gathered(),
            _resident((d, f)), _resident((d, f)), _resident((f, d)),
        ],
        out_specs=tok(),
        out_shape=jax.ShapeDtypeStruct((t, d), F32),
        compiler_params=_params("arbitrary"),
        name="final",
    )(ha, hb, x1, ada4, p["g_post2"], w8_t, yga, ygb, p["ws1"], p["ws3"], p["ws2"])


def kernel(x_prompt, x_sample, state_sconv, state_cconv, c_prompt, c_sample, w_ada, b_ada, g_pre1, g_post1,
           g_pre2, g_post2, w_in, w_sconv, w_out_a, w_cconv, b_cconv, ln_g, ln_b, w_out_b, b_out_b, w_o,
           w_router, e_bias, w1, w3, w2, ws1, ws3, ws2):
    assert w_ada.shape[0] == 1, "single-layer trunk"
    nb, seq, d = x_prompt.shape
    ns = x_sample.shape[0]
    n_prompt = nb * seq
    n_tokens = n_prompt + ns
    p = {
        "g_pre1": g_pre1, "g_post1": g_post1, "g_pre2": g_pre2, "g_post2": g_post2,
        "w_in": w_in[0].astype(BF16), "w_sconv": w_sconv[0], "w_out_a": w_out_a[0].astype(BF16),
        "w_cconv": w_cconv[0], "b_cconv": b_cconv, "ln_g": ln_g, "ln_b": ln_b,
        "w_out_b": w_out_b[0].astype(BF16), "b_out_b": b_out_b, "w_o": w_o[0].astype(BF16),
        "ws1": ws1[0].astype(BF16), "ws3": ws3[0].astype(BF16), "ws2": ws2[0].astype(BF16),
    }
    w_router_t = w_router[0].T.astype(BF16)
    e_bias_col = e_bias[0].reshape(N_EXPERTS, 1)

    ada = _ada(jnp.concatenate([c_prompt, c_sample], axis=0), w_ada[0], b_ada)
    ada = ada.reshape(nb + ns, 6, d).transpose(1, 0, 2)
    ada_p = ada[:, :nb].reshape(6, nb, 1, d)
    ada_s = ada[:, nb:].reshape(6, 1, ns, d)

    x1_p, ha, hb, u_tail, glu_tail = _prompt_mixer(x_prompt, ada_p, n_tokens, p)
    x1_s, ha, hb, new_sconv_sample, new_cconv_sample = _sample_mixer(
        x_sample.reshape(ns, d), ada_s, state_sconv[0], state_cconv[0], ha, hb, n_prompt, p)

    eid, rank, w8_t, cnt = _router(ha, hb, w_router_t, e_bias_col, ROUTER_TILE)
    counts = cnt[:, 0].astype(I32)
    n_tiles = (counts + EXPERT_TILE - 1) // EXPERT_TILE
    tile_ends = jnp.cumsum(n_tiles)
    first_tile = tile_ends - n_tiles
    n_rows = n_tokens * TOP_K + N_EXPERTS * EXPERT_COPY_ROWS
    tile_ids = jnp.arange(pl.cdiv(n_tokens * TOP_K, EXPERT_TILE) + N_EXPERTS, dtype=I32)
    tile_owner = jnp.minimum(jnp.sum(tile_ids[:, None] >= tile_ends[None, :], axis=1), N_EXPERTS - 1)
    tile_rows = jnp.clip(counts[tile_owner] - (tile_ids - first_tile[tile_owner]) * EXPERT_TILE, 0, EXPERT_TILE)
    tile_rows = jnp.where(tile_ids < tile_ends[-1], tile_rows, 0).astype(I32)
    tile_span = (tile_rows + EXPERT_COPY_ROWS - 1) // EXPERT_COPY_ROWS * EXPERT_COPY_ROWS
    tile_start = (jnp.cumsum(tile_span) - tile_span).astype(I32)
    dest = _positions(eid, rank, tile_start[first_tile])

    xa, xb = _dispatch_rows(ha, hb, dest, n_rows)
    ya, yb = _experts(xa, xb, first_tile, n_tiles, tile_ends[-1:], tile_rows, tile_start, w1[0], w3[0], w2[0])
    flat = dest.reshape(1, TOP_K * n_tokens)
    yga, ygb = (g.reshape(TOP_K, n_tokens, HALF) for g in _gather_rows(ya, yb, flat))

    y_p = _final(ha, hb, x1_p.reshape(n_prompt, d), ada_p, seq, w8_t, yga, ygb, 0, p, FINAL_TILE)
    y_s = _final(ha, hb, x1_s, ada_s, 1, w8_t, yga, ygb, n_prompt, p, ns)

    new_sconv_prompt = u_tail[:, SUBLANES - (K_SCONV - 1):][None]
    new_cconv_prompt = glu_tail[:, CONV_HALO - (K_CCONV - 1):][None]
    return (y_p.reshape(nb, seq, d), y_s.reshape(ns, 1, d),
            new_sconv_prompt, new_sconv_sample[None], new_cconv_prompt, new_cconv_sample[None])
```

```python
import functools

import jax
import jax.numpy as jnp
from jax import lax
from jax.experimental import pallas as pl
from jax.experimental.pallas import tpu as pltpu
from jax.experimental.pallas import tpu_sc as plsc

D_MODEL = 1024
K_SCONV = 3
K_CCONV = 31
N_EXPERTS = 64
TOP_K = 8
N_GROUPS = 8
GROUP_SIZE = N_EXPERTS // N_GROUPS
TOPK_GROUPS = 4
D_EXPERT = 256
ROUTED_SCALE = 2.5
RMS_EPS = 1e-6
LN_EPS = 1e-5

F32 = jnp.float32
BF16 = jnp.bfloat16
I32 = jnp.int32
NEG_INF = float("-inf")

VMEM_LIMIT_BYTES_V7X = 56 * 1024 * 1024
SUBLANES = 8
LANES = 128

MIXER_TILE = 512
CONV_HALO = 32
CONV_ROW_CHUNK = 64
CONV_LANE_CHUNK = 128
PROJ_LANE_CHUNK = 256
ROUTER_TILE = 384
POSITION_TILE = 5504
EXPERT_TILE = 768
EXPERT_COPY_ROWS = 256
EXPERT_RING = 4
FINAL_TILE = 512
PACKED = D_MODEL // 2
HALF = PACKED // 2
SC_WINDOW = 128


def _params(*semantics):
    return pltpu.CompilerParams(dimension_semantics=semantics, vmem_limit_bytes=VMEM_LIMIT_BYTES_V7X)


def _resident(shape):
    zeros = (0,) * len(shape)
    return pl.BlockSpec(shape, lambda *_: zeros, pipeline_mode=pl.Buffered(1))


def _sigmoid(x):
    return 0.5 * jnp.tanh(0.5 * x) + 0.5


def _silu(x):
    return x * _sigmoid(x)


def _rmsnorm(x, g):
    return x * lax.rsqrt(jnp.mean(x * x, axis=-1, keepdims=True) + RMS_EPS) * g


def _layernorm(x, g, b):
    mu = jnp.mean(x, axis=-1, keepdims=True)
    xc = x - mu
    var = jnp.mean(xc * xc, axis=-1, keepdims=True)
    return xc * lax.rsqrt(var + LN_EPS) * g + b


def _mm(a, w):
    return jnp.dot(a.astype(BF16), w.astype(BF16), preferred_element_type=F32)


def _pack_bf16_pairs(v):
    n = v.shape[1] // 2
    bits = lax.bitcast_convert_type(v.astype(BF16).astype(F32), I32)
    return bits[:, :n] | lax.shift_right_logical(bits[:, n:], 16)


def _unpack_bf16_pairs(words):
    hi = lax.bitcast_convert_type(words & jnp.int32(-65536), F32)
    lo = lax.bitcast_convert_type(lax.shift_left(words, 16), F32)
    return hi, lo


def _ada_kernel(c_ref, w_ref, b_ref, o_ref):
    o_ref[...] = _mm(_silu(c_ref[...]), w_ref[...]) + b_ref[...]


def _ada(c, w_ada, b_ada):
    n = c.shape[0]
    width = w_ada.shape[1]
    bn = D_MODEL
    return pl.pallas_call(
        _ada_kernel,
        grid=(width // bn,),
        in_specs=[
            pl.BlockSpec((n, D_MODEL), lambda i: (0, 0)),
            pl.BlockSpec((D_MODEL, bn), lambda i: (0, i)),
            pl.BlockSpec((1, bn), lambda i: (0, i)),
        ],
        out_specs=pl.BlockSpec((n, bn), lambda i: (0, i)),
        out_shape=jax.ShapeDtypeStruct((n, width), F32),
        compiler_params=_params("arbitrary"),
        name="ada",
    )(c, w_ada, b_ada)


def _mixer_tail(x, ada_ref, y_a, y_b, gate_a, gate_b, w_o_ref, gpost1_ref, gpre2_ref, x1_ref, ha_ref, hb_ref):
    m = gate_a * y_a + gate_b * y_b
    t = _mm(m, w_o_ref[...])
    x1 = x + ada_ref[2, 0] * _rmsnorm(t, gpost1_ref[...])
    h2 = _rmsnorm(x1, gpre2_ref[...]) * (1.0 + ada_ref[4, 0]) + ada_ref[3, 0]
    x1_ref[...] = x1.reshape(x1_ref.shape)
    words = _pack_bf16_pairs(h2)
    ha_ref[...] = words[:, :HALF].reshape(ha_ref.shape)
    hb_ref[...] = words[:, HALF:].reshape(hb_ref.shape)


def _prompt_mixer_kernel(*refs):
    live = pl.program_id(0) < pl.num_programs(0) - 1
    pl.when(live)(functools.partial(_prompt_mixer_body, *refs))

    @pl.when(jnp.logical_not(live))
    def _():
        ha_ref, hb_ref = refs[16], refs[17]
        ha_ref[...] = jnp.zeros(ha_ref.shape, I32)
        hb_ref[...] = jnp.zeros(hb_ref.shape, I32)


def _prompt_mixer_body(x_ref, ada_ref, gpre1_ref, gpost1_ref, gpre2_ref, w_in_ref, w_sconv_ref,
                       w_out_a_ref, w_cconv_ref, b_cconv_ref, ln_g_ref, ln_b_ref, w_out_b_ref,
                       b_out_b_ref, w_o_ref,
                       x1_ref, ha_ref, hb_ref, u_tail_ref, glu_tail_ref,
                       u_ext, glu_ext, cv_buf, h_buf, av_buf, ga_buf, gb_buf):
    tm = x_ref.shape[1]
    d = D_MODEL

    @pl.when(pl.program_id(1) == 0)
    def _():
        u_ext[pl.ds(0, SUBLANES), :] = jnp.zeros((SUBLANES, d), F32)
        glu_ext[pl.ds(0, CONV_HALO), :] = jnp.zeros((CONV_HALO, d), F32)

    h_buf[...] = (_rmsnorm(x_ref[0], gpre1_ref[...]) * (1.0 + ada_ref[1, 0]) + ada_ref[0, 0]).astype(BF16)

    def proj(i, lanes=slice(0, d)):
        return jnp.dot(h_buf[...], w_in_ref[:, i * d + lanes.start:i * d + lanes.stop],
                       preferred_element_type=F32)

    glu = proj(3) * _sigmoid(proj(4))
    glu_ext[pl.ds(CONV_HALO, tm), :] = glu
    glu_tail_ref[0] = glu[tm - CONV_HALO:, :]
    first = CONV_HALO - (K_CCONV - 1)

    def conv_piece(r0, lanes):
        acc = None
        for res in range(SUBLANES):
            rows = CONV_ROW_CHUNK + (SUBLANES if res else 0)
            part = None
            for k in range(K_CCONV):
                if (first + k) % SUBLANES != res:
                    continue
                start = r0 + first + k - res
                term = w_cconv_ref[k:k + 1, lanes] * glu_ext[start:start + rows, lanes]
                part = term if part is None else part + term
            part = part[res:res + CONV_ROW_CHUNK, :]
            acc = part if acc is None else acc + part
        cv_buf[r0:r0 + CONV_ROW_CHUNK, lanes] = acc

    def branch_a(lanes):
        u = proj(2, lanes) * proj(0, lanes)
        u_ext[pl.ds(SUBLANES, tm), lanes] = u
        v = (w_sconv_ref[0:1, lanes] * u_ext[pl.ds(SUBLANES - 2, tm), lanes]
             + w_sconv_ref[1:2, lanes] * u_ext[pl.ds(SUBLANES - 1, tm), lanes]
             + w_sconv_ref[2:3, lanes] * u)
        u_tail_ref[0, :, lanes] = u[tm - SUBLANES:, :]
        u_ext[pl.ds(0, SUBLANES), lanes] = u[tm - SUBLANES:, :]
        av_buf[:, lanes] = (proj(1, lanes) * v).astype(BF16)

    def gates(lanes):
        ga_buf[:, lanes] = _sigmoid(proj(5, lanes))
        gb_buf[:, lanes] = _sigmoid(proj(6, lanes))

    for c in range(0, d, PROJ_LANE_CHUNK):
        lanes = slice(c, c + PROJ_LANE_CHUNK)

        @pl.when(pl.program_id(1) + c >= 0)
        def _(lanes=lanes):
            for r0 in range(0, tm, CONV_ROW_CHUNK):
                for l0 in range(lanes.start, lanes.stop, CONV_LANE_CHUNK):
                    conv_piece(r0, slice(l0, l0 + CONV_LANE_CHUNK))
            branch_a(lanes)
            gates(lanes)

    glu_ext[pl.ds(0, CONV_HALO), :] = glu_ext[pl.ds(tm, CONV_HALO), :]

    y_a = jnp.dot(av_buf[...], w_out_a_ref[...], preferred_element_type=F32)
    cv = cv_buf[...] + b_cconv_ref[...]
    y_b = _mm(_silu(_layernorm(cv, ln_g_ref[...], ln_b_ref[...])), w_out_b_ref[...]) + b_out_b_ref[...]

    _mixer_tail(x_ref[0], ada_ref, y_a, y_b, ga_buf[...], gb_buf[...], w_o_ref, gpost1_ref, gpre2_ref,
                x1_ref, ha_ref, hb_ref)


def _prompt_mixer(x, ada4, n_tokens, p):
    nb, seq, d = x.shape
    tm = MIXER_TILE
    row = lambda: _resident((1, d))
    nj = seq // tm
    last = lambda b: jnp.minimum(b, nb - 1)
    tok = lambda w: pl.BlockSpec((1, tm, w), lambda b, j: (last(b), jnp.where(b < nb, j, nj - 1), 0))
    packed = lambda: pl.BlockSpec((tm, HALF), lambda b, j: (jnp.minimum(b * nj + j, nb * nj), 0))
    return pl.pallas_call(
        _prompt_mixer_kernel,
        grid=(nb + 1, nj),
        in_specs=[
            tok(d),
            pl.BlockSpec((6, 1, 1, d), lambda b, j: (0, last(b), 0, 0)),
            row(), row(), row(),
            _resident(p["w_in"].shape),
            _resident((K_SCONV, d)),
            _resident((d, d)),
            _resident((K_CCONV, d)),
            row(), row(), row(),
            _resident((d, d)),
            row(),
            _resident((d, d)),
        ],
        out_specs=[
            tok(d), packed(), packed(),
            pl.BlockSpec((1, SUBLANES, d), lambda b, j: (last(b), 0, 0)),
            pl.BlockSpec((1, CONV_HALO, d), lambda b, j: (last(b), 0, 0)),
        ],
        out_shape=[
            jax.ShapeDtypeStruct((nb, seq, d), F32),
            jax.ShapeDtypeStruct((n_tokens, HALF), I32),
            jax.ShapeDtypeStruct((n_tokens, HALF), I32),
            jax.ShapeDtypeStruct((nb, SUBLANES, d), F32),
            jax.ShapeDtypeStruct((nb, CONV_HALO, d), F32),
        ],
        scratch_shapes=[
            pltpu.VMEM((tm + SUBLANES, d), F32),
            pltpu.VMEM((tm + CONV_HALO + SUBLANES, d), F32),
            pltpu.VMEM((tm, d), F32),
            pltpu.VMEM((tm, d), BF16),
            pltpu.VMEM((tm, d), BF16),
            pltpu.VMEM((tm, d), F32),
            pltpu.VMEM((tm, d), F32),
        ],
        compiler_params=_params("arbitrary", "arbitrary"),
        name="prompt_mixer",
    )(x, ada4, p["g_pre1"], p["g_post1"], p["g_pre2"], p["w_in"], p["w_sconv"], p["w_out_a"],
      p["w_cconv"], p["b_cconv"], p["ln_g"], p["ln_b"], p["w_out_b"], p["b_out_b"], p["w_o"])


def _sample_mixer_kernel(x_ref, ada_ref, st_a_ref, st_b_ref, gpre1_ref, gpost1_ref, gpre2_ref, w_in_ref,
                         w_sconv_ref, w_out_a_ref, w_cconv_ref, b_cconv_ref, ln_g_ref, ln_b_ref,
                         w_out_b_ref, b_out_b_ref, w_o_ref, ha_all_ref, hb_all_ref,
                         x1_ref, ha_ref, hb_ref, new_a_ref, new_b_ref):
    del ha_all_ref, hb_all_ref
    d = D_MODEL
    x = x_ref[...]
    h = (_rmsnorm(x, gpre1_ref[...]) * (1.0 + ada_ref[1, 0]) + ada_ref[0, 0]).astype(BF16)

    def proj(i):
        return jnp.dot(h, w_in_ref[:, i * d:(i + 1) * d], preferred_element_type=F32)

    u = proj(2) * proj(0)
    v = w_sconv_ref[K_SCONV - 1:K_SCONV, :] * u
    for k in range(K_SCONV - 1):
        v = v + w_sconv_ref[k:k + 1, :] * st_a_ref[:, k, :]
    new_a_ref[:, :K_SCONV - 2, :] = st_a_ref[:, 1:, :]
    new_a_ref[:, K_SCONV - 2, :] = u
    y_a = _mm(proj(1) * v, w_out_a_ref[...])

    glu = proj(3) * _sigmoid(proj(4))
    cv = w_cconv_ref[K_CCONV - 1:K_CCONV, :] * glu + b_cconv_ref[...]
    for k in range(K_CCONV - 1):
        cv = cv + w_cconv_ref[k:k + 1, :] * st_b_ref[:, k, :]
    new_b_ref[:, :K_CCONV - 2, :] = st_b_ref[:, 1:, :]
    new_b_ref[:, K_CCONV - 2, :] = glu
    y_b = _mm(_silu(_layernorm(cv, ln_g_ref[...], ln_b_ref[...])), w_out_b_ref[...]) + b_out_b_ref[...]

    _mixer_tail(x, ada_ref, y_a, y_b, _sigmoid(proj(5)), _sigmoid(proj(6)), w_o_ref, gpost1_ref,
                gpre2_ref, x1_ref, ha_ref, hb_ref)


def _sample_mixer(x, ada4, st_a, st_b, ha_all, hb_all, first_token, p):
    n, d = x.shape
    tb = 32
    first_block = first_token // tb
    row = lambda: _resident((1, d))
    tok = lambda w: pl.BlockSpec((tb, w), lambda i: (i, 0))
    state = lambda k: pl.BlockSpec((tb, k - 1, d), lambda i: (i, 0, 0))
    packed = lambda: pl.BlockSpec((tb, HALF), lambda i: (first_block + i, 0))
    return pl.pallas_call(
        _sample_mixer_kernel,
        grid=(n // tb,),
        in_specs=[
            tok(d),
            pl.BlockSpec((6, 1, tb, d), lambda i: (0, 0, i, 0)),
            state(K_SCONV), state(K_CCONV),
            row(), row(), row(),
            _resident(p["w_in"].shape),
            _resident((K_SCONV, d)),
            _resident((d, d)),
            _resident((K_CCONV, d)),
            row(), row(), row(),
            _resident((d, d)),
            row(),
            _resident((d, d)),
            pl.BlockSpec(memory_space=pl.ANY), pl.BlockSpec(memory_space=pl.ANY),
        ],
        out_specs=[tok(d), packed(), packed(), state(K_SCONV), state(K_CCONV)],
        out_shape=[
            jax.ShapeDtypeStruct((n, d), F32),
            jax.ShapeDtypeStruct(ha_all.shape, I32),
            jax.ShapeDtypeStruct(hb_all.shape, I32),
            jax.ShapeDtypeStruct(st_a.shape, F32),
            jax.ShapeDtypeStruct(st_b.shape, F32),
        ],
        input_output_aliases={17: 1, 18: 2},
        compiler_params=_params("arbitrary"),
        name="sample_mixer",
    )(x, ada4, st_a, st_b, p["g_pre1"], p["g_post1"], p["g_pre2"], p["w_in"], p["w_sconv"], p["w_out_a"],
      p["w_cconv"], p["b_cconv"], p["ln_g"], p["ln_b"], p["w_out_b"], p["b_out_b"], p["w_o"], ha_all, hb_all)


def _first_index_of_max(work, iota, axis, limit):
    m = jnp.max(work, axis=axis, keepdims=True)
    return m, jnp.min(jnp.where(work == m, iota, limit), axis=axis, keepdims=True)


def _router_kernel(ha_ref, hb_ref, wr_ref, bias_ref, eid_ref, rank_ref, w8_ref, cnt_ref):
    tm = ha_ref.shape[0]
    hi, lo = _unpack_bf16_pairs(jnp.concatenate([ha_ref[...], hb_ref[...]], axis=1))
    nt = (((1,), (1,)), ((), ()))
    logits = (lax.dot_general(wr_ref[:, :PACKED], hi.astype(BF16), nt, preferred_element_type=F32)
              + lax.dot_general(wr_ref[:, PACKED:], lo.astype(BF16), nt, preferred_element_type=F32))
    scores = _sigmoid(logits)
    sel = scores + bias_ref[...]

    sel3 = sel.reshape(N_GROUPS, GROUP_SIZE, tm)
    io3 = lax.broadcasted_iota(I32, sel3.shape, 1)
    m1, i1 = _first_index_of_max(sel3, io3, 1, GROUP_SIZE)
    m2 = jnp.max(jnp.where(io3 == i1, NEG_INF, sel3), axis=1, keepdims=True)
    gscore = (m1 + m2).reshape(N_GROUPS, tm)

    iog = lax.broadcasted_iota(I32, gscore.shape, 0)
    gmask = jnp.zeros(gscore.shape, jnp.bool_)
    for _ in range(TOPK_GROUPS):
        _, gi = _first_index_of_max(gscore, iog, 0, N_GROUPS)
        pick = iog == gi
        gmask = jnp.logical_or(gmask, pick)
        gscore = jnp.where(pick, NEG_INF, gscore)

    work = jnp.where(gmask.reshape(N_GROUPS, 1, tm), sel3, NEG_INF).reshape(N_EXPERTS, tm)
    ioe = lax.broadcasted_iota(I32, work.shape, 0)
    chosen = jnp.zeros(work.shape, jnp.bool_)
    eids, ws, picks = [], [], []
    for _ in range(TOP_K):
        _, ei = _first_index_of_max(work, ioe, 0, N_EXPERTS)
        pick = ioe == ei
        eids.append(ei)
        picks.append(pick)
        ws.append(jnp.sum(jnp.where(pick, scores, 0.0), axis=0, keepdims=True))
        chosen = jnp.logical_or(chosen, pick)
        work = jnp.where(pick, NEG_INF, work)

    w = jnp.concatenate(ws, axis=0)
    eid_ref[...] = jnp.concatenate(eids, axis=0)
    w8_ref[...] = (w / jnp.sum(w, axis=0, keepdims=True) * ROUTED_SCALE).T

    @pl.when(pl.program_id(0) == 0)
    def _():
        cnt_ref[...] = jnp.zeros(cnt_ref.shape, F32)

    before = lax.broadcasted_iota(I32, (tm, tm), 0) < lax.broadcasted_iota(I32, (tm, tm), 1)
    seen = cnt_ref[:, 0:1] + jnp.dot(chosen.astype(BF16), before.astype(BF16), preferred_element_type=F32)
    rank_ref[...] = jnp.concatenate(
        [jnp.sum(jnp.where(pick, seen, 0.0), axis=0, keepdims=True) for pick in picks], axis=0)
    cnt_ref[...] += jnp.sum(chosen.astype(F32), axis=1, keepdims=True)


def _router(ha, hb, w_router_t, e_bias_col, tm):
    t = ha.shape[0]
    d = D_MODEL
    picks = lambda: pl.BlockSpec((TOP_K, tm), lambda i: (0, i))
    half_rows = lambda: pl.BlockSpec((tm, HALF), lambda i: (i, 0))
    return pl.pallas_call(
        _router_kernel,
        grid=(t // tm,),
        in_specs=[
            half_rows(), half_rows(),
            _resident((N_EXPERTS, d)),
            _resident((N_EXPERTS, 1)),
        ],
        out_specs=[picks(), picks(), pl.BlockSpec((tm, TOP_K), lambda i: (i, 0)),
                   pl.BlockSpec((N_EXPERTS, LANES), lambda i: (0, 0))],
        out_shape=[
            jax.ShapeDtypeStruct((TOP_K, t), I32),
            jax.ShapeDtypeStruct((TOP_K, t), F32),
            jax.ShapeDtypeStruct((t, TOP_K), F32),
            jax.ShapeDtypeStruct((N_EXPERTS, LANES), F32),
        ],
        compiler_params=_params("arbitrary"),
        name="router",
    )(ha, hb, w_router_t, e_bias_col)


def _offset_kernel(off_ref, eid_ref, rank_ref, dest_ref):
    eid = eid_ref[...]
    dest = rank_ref[...].astype(I32)
    for e in range(N_EXPERTS):
        dest = dest + jnp.where(eid == e, off_ref[e], 0)
    dest_ref[...] = dest


def _positions(eid, rank, first_row):
    t = eid.shape[1]
    tm = POSITION_TILE
    picks = lambda: pl.BlockSpec((TOP_K, tm), lambda i, off: (0, i))
    return pl.pallas_call(
        _offset_kernel,
        grid_spec=pltpu.PrefetchScalarGridSpec(
            num_scalar_prefetch=1, grid=(t // tm,), in_specs=[picks(), picks()], out_specs=picks()),
        out_shape=jax.ShapeDtypeStruct((TOP_K, t), I32),
        compiler_params=_params("arbitrary"),
        name="positions",
    )(first_row, eid, rank)


def _sc_mesh():
    return plsc.VectorSubcoreMesh(core_axis_name="core", subcore_axis_name="subcore")


def _dispatch_rows(xa, xb, dest, n_rows):
    t, width = xa.shape
    n_picks = dest.shape[0]
    out = jax.ShapeDtypeStruct((n_rows, width), xa.dtype)

    @functools.partial(pl.kernel, out_type=[out, out], mesh=_sc_mesh(), scratch_types=[])
    def scatter(xa_hbm, xb_hbm, i_hbm, oa_hbm, ob_hbm):
        for x_hbm, o_hbm in ((xa_hbm, oa_hbm), (xb_hbm, ob_hbm)):
            def body(x_vmem, i_vmem, o_hbm=o_hbm):
                for k in range(n_picks):
                    pltpu.sync_copy(x_vmem, o_hbm.at[i_vmem.at[k]])

            pltpu.emit_pipeline(
                body,
                grid=(t // SC_WINDOW,),
                in_specs=[pl.BlockSpec((SC_WINDOW, width), lambda i: (i, 0)),
                          pl.BlockSpec((n_picks, SC_WINDOW), lambda i: (0, i))],
                out_specs=[],
                core_axis_name=("core", "subcore"),
                dimension_semantics=(pltpu.PARALLEL,),
            )(x_hbm, i_hbm)

    return scatter(xa, xb, dest)


def _gather_rows(ya, yb, idx):
    n = idx.shape[1]
    width = ya.shape[1]
    out = jax.ShapeDtypeStruct((n, width), ya.dtype)

    @functools.partial(pl.kernel, out_type=[out, out], mesh=_sc_mesh(), scratch_types=[])
    def gather(ya_hbm, yb_hbm, i_hbm, oa_hbm, ob_hbm):
        for y_hbm, o_hbm in ((ya_hbm, oa_hbm), (yb_hbm, ob_hbm)):
            def body(i_vmem, o_vmem, y_hbm=y_hbm):
                pltpu.sync_copy(y_hbm.at[i_vmem.at[0]], o_vmem)

            pltpu.emit_pipeline(
                body,
                grid=(n // SC_WINDOW,),
                in_specs=[pl.BlockSpec((1, SC_WINDOW), lambda i: (0, i))],
                out_specs=[pl.BlockSpec((SC_WINDOW, width), lambda i: (i, 0))],
                core_axis_name=("core", "subcore"),
                dimension_semantics=(pltpu.PARALLEL,),
            )(i_hbm, o_hbm)

    return gather(ya, yb, idx)


def _expert_kernel(first_tile_ref, n_tiles_ref, total_ref, tile_rows_ref, tile_start_ref, xa_hbm, xb_hbm, w1_ref,
                   w3_ref, w2_ref,
                   ya_hbm, yb_hbm, xa_buf, xb_buf, ya_buf, yb_buf, in_sem, out_sem, w1_bf, w3_bf, w2_bf):
    e = pl.program_id(0)
    tm = EXPERT_TILE
    ring = EXPERT_RING
    total = total_ref[0]
    first_tile = first_tile_ref[e]

    def pieces(g, slot, bufs_hbm, sems, to_vmem):
        out = []
        for r0 in range(0, tm, EXPERT_COPY_ROWS):
            hbm_rows = pl.ds(pl.multiple_of(tile_start_ref[g] + r0, EXPERT_COPY_ROWS), EXPERT_COPY_ROWS)
            for i, (buf, hbm) in enumerate(bufs_hbm):
                vmem = buf.at[slot, pl.ds(r0, EXPERT_COPY_ROWS)]
                src, dst = (hbm.at[hbm_rows], vmem) if to_vmem else (vmem, hbm.at[hbm_rows])
                out.append((r0, pltpu.make_async_copy(src, dst, sems.at[i, slot])))
        return out

    def for_live_pieces(g, bufs_hbm, sems, to_vmem, act):
        slot = g % ring
        live_rows = tile_rows_ref[g]
        for r0, copy in pieces(g, slot, bufs_hbm, sems, to_vmem):
            pl.when(r0 < live_rows)(functools.partial(act, copy))

    ins = ((xa_buf, xa_hbm), (xb_buf, xb_hbm))
    outs = ((ya_buf, ya_hbm), (yb_buf, yb_hbm))
    start_in = lambda g: for_live_pieces(g, ins, in_sem, True, lambda c: c.start())
    wait_in = lambda g: for_live_pieces(g, ins, in_sem, True, lambda c: c.wait())
    start_out = lambda g: for_live_pieces(g, outs, out_sem, False, lambda c: c.start())
    wait_out = lambda g: for_live_pieces(g, outs, out_sem, False, lambda c: c.wait())

    @pl.when(n_tiles_ref[e] > 0)
    def _():
        w1_bf[...] = w1_ref[0].astype(BF16)
        w3_bf[...] = w3_ref[0].astype(BF16)
        w2_bf[...] = w2_ref[0].astype(BF16)

    def tile(t, carry):
        g = first_tile + t
        slot = g % ring

        @pl.when(g == 0)
        def _():
            for ahead in range(ring - 1):
                pl.when(ahead < total)(functools.partial(start_in, ahead))

        pl.when(g + ring - 1 < total)(functools.partial(start_in, g + ring - 1))
        wait_in(g)
        pl.when(g >= ring)(functools.partial(wait_out, g - ring))

        def compute(m):
            rows = pl.ds(0, m)
            hi, lo = _unpack_bf16_pairs(jnp.concatenate([xa_buf[slot, rows], xb_buf[slot, rows]], axis=1))
            hi = hi.astype(BF16)
            lo = lo.astype(BF16)

            def up(w_bf):
                return (jnp.dot(hi, w_bf[:PACKED, :], preferred_element_type=F32)
                        + jnp.dot(lo, w_bf[PACKED:, :], preferred_element_type=F32))

            act = _silu(up(w1_bf)) * up(w3_bf)
            words = _pack_bf16_pairs(_mm(act, w2_bf[...]))
            ya_buf[slot, rows] = words[:, :HALF]
            yb_buf[slot, rows] = words[:, HALF:]

        live_rows = tile_rows_ref[g]
        for m in range(EXPERT_COPY_ROWS, tm + 1, EXPERT_COPY_ROWS):
            pl.when(jnp.logical_and(live_rows > m - EXPERT_COPY_ROWS, live_rows <= m))(
                functools.partial(compute, m))
        start_out(g)
        return carry

    lax.fori_loop(0, n_tiles_ref[e], tile, 0)

    @pl.when(e == pl.num_programs(0) - 1)
    def _():
        for back in range(ring, 0, -1):
            pl.when(total >= back)(functools.partial(wait_out, total - back))


def _experts(xa, xb, first_tile, n_tiles, total_tiles, tile_rows, tile_start, w1, w3, w2):
    rows = xa.shape[0]
    d, f = D_MODEL, D_EXPERT
    tm = EXPERT_TILE
    weights = lambda a, b: pl.BlockSpec((1, a, b), lambda e, *_: (e, 0, 0))
    ring = lambda: pltpu.VMEM((EXPERT_RING, tm, HALF), I32)
    sems = lambda: pltpu.SemaphoreType.DMA((2, EXPERT_RING))
    return pl.pallas_call(
        _expert_kernel,
        grid_spec=pltpu.PrefetchScalarGridSpec(
            num_scalar_prefetch=5,
            grid=(N_EXPERTS,),
            in_specs=[
                pl.BlockSpec(memory_space=pl.ANY), pl.BlockSpec(memory_space=pl.ANY),
                weights(d, f), weights(d, f), weights(f, d),
            ],
            out_specs=[pl.BlockSpec(memory_space=pl.ANY), pl.BlockSpec(memory_space=pl.ANY)],
            scratch_shapes=[
                ring(), ring(), ring(), ring(), sems(), sems(),
                pltpu.VMEM((d, f), BF16), pltpu.VMEM((d, f), BF16), pltpu.VMEM((f, d), BF16),
            ],
        ),
        out_shape=[jax.ShapeDtypeStruct((rows, HALF), I32), jax.ShapeDtypeStruct((rows, HALF), I32)],
        compiler_params=_params("arbitrary"),
        name="experts",
    )(first_tile, n_tiles, total_tiles, tile_rows, tile_start, xa, xb, w1, w3, w2)


def _final_kernel(ha_ref, hb_ref, x1_ref, ada_ref, gpost2_ref, w8_ref, yga_ref, ygb_ref, ws1_ref, ws3_ref,
                  ws2_ref, out_ref):
    hi, lo = _unpack_bf16_pairs(jnp.concatenate([ha_ref[...], hb_ref[...]], axis=1))
    hi = hi.astype(BF16)
    lo = lo.astype(BF16)

    def up(w_ref):
        return (jnp.dot(hi, w_ref[:PACKED, :], preferred_element_type=F32)
                + jnp.dot(lo, w_ref[PACKED:, :], preferred_element_type=F32))

    f = _mm(_silu(up(ws1_ref)) * up(ws3_ref), ws2_ref[...])
    for k in range(TOP_K):
        hi, lo = _unpack_bf16_pairs(jnp.concatenate([yga_ref[k], ygb_ref[k]], axis=1))
        f = f + w8_ref[:, k:k + 1] * jnp.concatenate([hi, lo], axis=1)
    out_ref[...] = x1_ref[...] + ada_ref[5, 0] * _rmsnorm(f, gpost2_ref[...])


def _final(ha, hb, x1, ada4, tokens_per_ada_block, w8_t, yga, ygb, first_token, p, tm):
    t, d = x1.shape
    f = D_EXPERT
    first_block = first_token // tm
    tok = lambda: pl.BlockSpec((tm, d), lambda i: (i, 0))
    if ada4.shape[2] == 1:
        ada_spec = pl.BlockSpec((6, 1, 1, d), lambda i: (0, i * tm // tokens_per_ada_block, 0, 0))
    else:
        ada_spec = pl.BlockSpec((6, 1, tm, d), lambda i: (0, 0, i, 0))
    gathered = lambda: pl.BlockSpec((TOP_K, tm, HALF), lambda i: (0, first_block + i, 0))
    return pl.pallas_call(
        _final_kernel,
        grid=(t // tm,),
        in_specs=[
            pl.BlockSpec((tm, HALF), lambda i: (first_block + i, 0)),
            pl.BlockSpec((tm, HALF), lambda i: (first_block + i, 0)),
            tok(), ada_spec, _resident((1, d)),
            pl.BlockSpec((tm, TOP_K), lambda i: (first_block + i, 0)),
            gathered(), gathered(),
            _resident((d, f)), _resident((d, f)), _resident((f, d)),
        ],
        out_specs=tok(),
        out_shape=jax.ShapeDtypeStruct((t, d), F32),
        compiler_params=_params("arbitrary"),
        name="final",
    )(ha, hb, x1, ada4, p["g_post2"], w8_t, yga, ygb, p["ws1"], p["ws3"], p["ws2"])


def kernel(x_prompt, x_sample, state_sconv, state_cconv, c_prompt, c_sample, w_ada, b_ada, g_pre1, g_post1,
           g_pre2, g_post2, w_in, w_sconv, w_out_a, w_cconv, b_cconv, ln_g, ln_b, w_out_b, b_out_b, w_o,
           w_router, e_bias, w1, w3, w2, ws1, ws3, ws2):
    assert w_ada.shape[0] == 1, "single-layer trunk"
    nb, seq, d = x_prompt.shape
    ns = x_sample.shape[0]
    n_prompt = nb * seq
    n_tokens = n_prompt + ns
    p = {
        "g_pre1": g_pre1, "g_post1": g_post1, "g_pre2": g_pre2, "g_post2": g_post2,
        "w_in": w_in[0].astype(BF16), "w_sconv": w_sconv[0], "w_out_a": w_out_a[0].astype(BF16),
        "w_cconv": w_cconv[0], "b_cconv": b_cconv, "ln_g": ln_g, "ln_b": ln_b,
        "w_out_b": w_out_b[0].astype(BF16), "b_out_b": b_out_b, "w_o": w_o[0].astype(BF16),
        "ws1": ws1[0].astype(BF16), "ws3": ws3[0].astype(BF16), "ws2": ws2[0].astype(BF16),
    }
    w_router_t = w_router[0].T.astype(BF16)
    e_bias_col = e_bias[0].reshape(N_EXPERTS, 1)

    ada = _ada(jnp.concatenate([c_prompt, c_sample], axis=0), w_ada[0], b_ada)
    ada = ada.reshape(nb + ns, 6, d).transpose(1, 0, 2)
    ada_p = ada[:, :nb].reshape(6, nb, 1, d)
    ada_s = ada[:, nb:].reshape(6, 1, ns, d)

    x1_p, ha, hb, u_tail, glu_tail = _prompt_mixer(x_prompt, ada_p, n_tokens, p)
    x1_s, ha, hb, new_sconv_sample, new_cconv_sample = _sample_mixer(
        x_sample.reshape(ns, d), ada_s, state_sconv[0], state_cconv[0], ha, hb, n_prompt, p)

    eid, rank, w8_t, cnt = _router(ha, hb, w_router_t, e_bias_col, ROUTER_TILE)
    counts = cnt[:, 0].astype(I32)
    n_tiles = (counts + EXPERT_TILE - 1) // EXPERT_TILE
    tile_ends = jnp.cumsum(n_tiles)
    first_tile = tile_ends - n_tiles
    n_rows = n_tokens * TOP_K + N_EXPERTS * EXPERT_COPY_ROWS
    tile_ids = jnp.arange(pl.cdiv(n_tokens * TOP_K, EXPERT_TILE) + N_EXPERTS, dtype=I32)
    tile_owner = jnp.minimum(jnp.sum(tile_ids[:, None] >= tile_ends[None, :], axis=1), N_EXPERTS - 1)
    tile_rows = jnp.clip(counts[tile_owner] - (tile_ids - first_tile[tile_owner]) * EXPERT_TILE, 0, EXPERT_TILE)
    tile_rows = jnp.where(tile_ids < tile_ends[-1], tile_rows, 0).astype(I32)
    tile_span = (tile_rows + EXPERT_COPY_ROWS - 1) // EXPERT_COPY_ROWS * EXPERT_COPY_ROWS
    tile_start = (jnp.cumsum(tile_span) - tile_span).astype(I32)
    dest = _positions(eid, rank, tile_start[first_tile])

    xa, xb = _dispatch_rows(ha, hb, dest, n_rows)
    ya, yb = _experts(xa, xb, first_tile, n_tiles, tile_ends[-1:], tile_rows, tile_start, w1[0], w3[0], w2[0])
    flat = dest.reshape(1, TOP_K * n_tokens)
    yga, ygb = (g.reshape(TOP_K, n_tokens, HALF) for g in _gather_rows(ya, yb, flat))

    y_p = _final(ha, hb, x1_p.reshape(n_prompt, d), ada_p, seq, w8_t, yga, ygb, 0, p, FINAL_TILE)
    y_s = _final(ha, hb, x1_s, ada_s, 1, w8_t, yga, ygb, n_prompt, p, ns)

    new_sconv_prompt = u_tail[:, SUBLANES - (K_SCONV - 1):][None]
    new_cconv_prompt = glu_tail[:, CONV_HALO - (K_CCONV - 1):][None]
    return (y_p.reshape(nb, seq, d), y_s.reshape(ns, 1, d),
            new_sconv_prompt, new_sconv_sample[None], new_cconv_prompt, new_cconv_sample[None])
```
